```python
import math
import jax, jax.numpy as jnp
from jax import lax
import numpy as np

D_MODEL = 1024
BATCH = 16
SEQ = 256
DEPTH = 4
DEC_BATCH = 2
DEC_SEQ = 1024
PAST_LEN = 512

GRID_W = 64
EPS = 1e-6
NEG_INF = -1e30
CONV_W = 5
CHUNK = 64

SSD_HEADS = 4
SSD_HEAD_DIM = 64
SSD_WIDTH = SSD_HEADS * SSD_HEAD_DIM
SSD_GROUPS = 2
SSD_STATE = 64
SSD_CONV_CH = SSD_WIDTH + 2 * SSD_GROUPS * SSD_STATE
SSD_COLS = SSD_WIDTH + SSD_CONV_CH + 2 * SSD_HEADS
GDN_HEADS = 4
GDN_HEAD_DIM = 64
GDN_WIDTH = GDN_HEADS * GDN_HEAD_DIM
GDN_COLS = 4 * GDN_WIDTH + 4 * GDN_HEADS
FNET_GROUPS = 4
FNET_GROUP_DIM = 64
FNET_WIDTH = FNET_GROUPS * FNET_GROUP_DIM
FNET_COLS = FNET_WIDTH
ATT_HEADS = 4
ATT_KV_HEADS = 2
ATT_REP = ATT_HEADS // ATT_KV_HEADS
ATT_HEAD_DIM = 64
ATT_WIDTH = ATT_HEADS * ATT_HEAD_DIM
ATT_COLS = ATT_WIDTH + 2 * ATT_KV_HEADS * ATT_HEAD_DIM
WINDOW = 128
ABLOCK = 128
ROPE_THETA = 10000.0

MIX_WIDTH = SSD_WIDTH + GDN_WIDTH + FNET_WIDTH + ATT_WIDTH
IN_COLS = SSD_COLS + GDN_COLS + FNET_COLS + ATT_COLS

PEER_HEADS = 8
PEER_NKEYS = 128
PEER_EXPERTS = PEER_NKEYS * PEER_NKEYS
PEER_QDIM = 256
PEER_TOPK = 16
PEER_TOKEN_BLOCK = 128

kernel_name = 'hybrid_diffusion_peer_trunk_step'


def rmsnorm(x, w):
    xf = x.astype(jnp.float32)
    y = xf * lax.rsqrt(jnp.mean(xf * xf, axis=-1, keepdims=True) + EPS)
    return (y * w.astype(jnp.float32)).astype(x.dtype)


def l2norm(x):
    return x * lax.rsqrt(jnp.sum(x * x, axis=-1, keepdims=True) + EPS)


def adaln(cvec, w_mod, b_mod):
    m = jax.nn.silu(cvec) @ w_mod + b_mod
    return jnp.split(m[:, None, :], 6, axis=-1)


def dwconv_centred(x, w, b):
    ch = x.shape[-1]
    pad = (CONV_W - 1) // 2
    y = lax.conv_general_dilated(x, w[:, None, :].astype(x.dtype), window_strides=(1,),
                                 padding=[(pad, pad)], dimension_numbers=('NWC', 'WIO', 'NWC'),
                                 feature_group_count=ch)
    return y + b.astype(x.dtype)


def ssd_scan(x, dt, A, Bm, Cm, h0):
    b, l, h, p = x.shape
    n = Bm.shape[-1]
    nc = l // CHUNK
    a_cum = jnp.cumsum((dt * A).reshape(b, nc, CHUNK, h), axis=2)
    xdt = (x * dt[..., None]).reshape(b, nc, CHUNK, h, p)
    Bc = Bm.reshape(b, nc, CHUNK, h, n)
    Cc = Cm.reshape(b, nc, CHUNK, h, n)
    tri = jnp.tril(jnp.ones((CHUNK, CHUNK), dtype=bool))[:, :, None]
    seg = a_cum[:, :, :, None, :] - a_cum[:, :, None, :, :]
    Lmat = jnp.where(tri, jnp.exp(jnp.where(tri, seg, 0.0)), 0.0)
    y_diag = jnp.einsum('bcqhn,bckhn,bcqkh,bckhp->bcqhp', Cc, Bc, Lmat, xdt)
    decay_to_end = jnp.exp(a_cum[:, :, -1:, :] - a_cum)
    chunk_states = jnp.einsum('bckhn,bckh,bckhp->bchpn', Bc, decay_to_end, xdt)
    chunk_decay = jnp.exp(a_cum[:, :, -1, :])

    def step(state, inp):
        st, dec = inp
        return state * dec[:, :, None, None] + st, state

    h_final, h_enter = lax.scan(step, h0, (jnp.moveaxis(chunk_states, 1, 0),
                                           jnp.moveaxis(chunk_decay, 1, 0)))
    h_enter = jnp.moveaxis(h_enter, 0, 1)
    y_off = jnp.einsum('bcqhn,bchpn,bcqh->bcqhp', Cc, h_enter, jnp.exp(a_cum))
    return (y_diag + y_off).reshape(b, l, h, p), h_final


def ssd_mixer(u, conv_w, conv_b, A_log, dt_bias, D_skip, norm_w, h0):
    b, l, _ = u.shape
    f32 = jnp.float32
    z, xbc, dt_raw = jnp.split(u, [SSD_WIDTH, SSD_WIDTH + SSD_CONV_CH], axis=-1)
    xbc = jax.nn.silu(dwconv_centred(xbc, conv_w, conv_b)).astype(f32)
    xs, Bm, Cm = jnp.split(xbc, [SSD_WIDTH, SSD_WIDTH + SSD_GROUPS * SSD_STATE], axis=-1)
    xs = xs.reshape(b, l, SSD_HEADS, SSD_HEAD_DIM)
    rep = SSD_HEADS // SSD_GROUPS
    Bm = jnp.repeat(Bm.reshape(b, l, SSD_GROUPS, SSD_STATE), rep, axis=2)
    Cm = jnp.repeat(Cm.reshape(b, l, SSD_GROUPS, SSD_STATE), rep, axis=2)
    dt = jax.nn.softplus(dt_raw.astype(f32).reshape(b, l, 2, SSD_HEADS) + dt_bias.astype(f32))
    A = -jnp.exp(A_log.astype(f32))
    h0 = h0.astype(f32)
    y_f, h_f = ssd_scan(xs, dt[:, :, 0], A[0], Bm, Cm, h0[:, 0])
    y_b, h_b = ssd_scan(xs[:, ::-1], dt[:, ::-1, 1], A[1], Bm[:, ::-1], Cm[:, ::-1], h0[:, 1])
    y = y_f + y_b[:, ::-1] + D_skip.astype(f32)[:, None] * xs
    y = y * jax.nn.silu(z.astype(f32)).reshape(b, l, SSD_HEADS, SSD_HEAD_DIM)
    y = rmsnorm(y, norm_w.reshape(SSD_HEADS, SSD_HEAD_DIM))
    return y.reshape(b, l, SSD_WIDTH).astype(u.dtype), jnp.stack([h_f, h_b], axis=1)


def gdn_scan(q, k, v, log_alpha, beta, S0):
    b, l, h, dk = q.shape
    dv = v.shape[-1]
    nc = l // CHUNK

    def chunked(t):
        return jnp.moveaxis(t.reshape((b, nc, CHUNK) + t.shape[2:]), 2, 3)

    qc = chunked(q) * (dk ** -0.5)
    kc, vc = chunked(k), chunked(v)
    gcum = jnp.cumsum(chunked(log_alpha), axis=-1)
    bc = chunked(beta)
    tri = jnp.tril(jnp.ones((CHUNK, CHUNK), dtype=bool))
    strict = jnp.tril(jnp.ones((CHUNK, CHUNK), dtype=bool), k=-1)
    seg = gcum[..., :, None] - gcum[..., None, :]
    decay = jnp.where(tri, jnp.exp(jnp.where(tri, seg, 0.0)), 0.0)
    kk = jnp.einsum('bchid,bchjd->bchij', kc, kc)
    A = jnp.where(strict, bc[..., :, None] * kk * decay, 0.0)
    rhs = jnp.concatenate([vc * bc[..., None], kc * (bc * jnp.exp(gcum))[..., None]], axis=-1)
    sol = lax.linalg.triangular_solve(jnp.eye(CHUNK, dtype=A.dtype) + A, rhs,
                                      left_side=True, lower=True)
    u_c, w_c = sol[..., :dv], sol[..., dv:]
    attn = jnp.where(tri, jnp.einsum('bchid,bchjd->bchij', qc, kc) * decay, 0.0)
    q_dec = qc * jnp.exp(gcum)[..., None]
    k_end = kc * jnp.exp(gcum[..., -1:] - gcum)[..., None]
    c_dec = jnp.exp(gcum[..., -1])

    def step(S, inp):
        u_i, w_i, a_i, qd_i, ke_i, d_i = inp
        v_new = u_i - jnp.einsum('bhqk,bhkv->bhqv', w_i, S)
        o = jnp.einsum('bhqk,bhkv->bhqv', qd_i, S) + jnp.einsum('bhij,bhjv->bhiv', a_i, v_new)
        S = S * d_i[..., None, None] + jnp.einsum('bhqk,bhqv->bhkv', ke_i, v_new)
        return S, o

    xs = tuple(jnp.moveaxis(t, 1, 0) for t in (u_c, w_c, attn, q_dec, k_end, c_dec))
    S_final, o = lax.scan(step, S0, xs)
    o = jnp.transpose(o, (1, 0, 3, 2, 4)).reshape(b, l, h, dv)
    return o, S_final


def gdn_mixer(u, conv_w, conv_b, A_log, dt_bias, norm_w, S0):
    b, l, _ = u.shape
    f32 = jnp.float32
    qkv, gate, a_raw, b_raw = jnp.split(
        u, [3 * GDN_WIDTH, 4 * GDN_WIDTH, 4 * GDN_WIDTH + 2 * GDN_HEADS], axis=-1)
    qkv = jax.nn.silu(dwconv_centred(qkv, conv_w, conv_b)).astype(f32)
    q, k, v = [t.reshape(b, l, GDN_HEADS, GDN_HEAD_DIM) for t in jnp.split(qkv, 3, axis=-1)]
    q, k = l2norm(q), l2norm(k)
    log_alpha = -jnp.exp(A_log.astype(f32)) * jax.nn.softplus(
        a_raw.astype(f32).reshape(b, l, 2, GDN_HEADS) + dt_bias.astype(f32))
    beta = jax.nn.sigmoid(b_raw.astype(f32).reshape(b, l, 2, GDN_HEADS))
    S0 = S0.astype(f32)
    o_f, S_f = gdn_scan(q, k, v, log_alpha[:, :, 0], beta[:, :, 0], S0[:, 0])
    o_b, S_b = gdn_scan(q[:, ::-1], k[:, ::-1], v[:, ::-1], log_alpha[:, ::-1, 1],
                        beta[:, ::-1, 1], S0[:, 1])
    o = rmsnorm(o_f + o_b[:, ::-1], norm_w) * jax.nn.silu(gate.astype(f32)).reshape(
        b, l, GDN_HEADS, GDN_HEAD_DIM)
    return o.reshape(b, l, GDN_WIDTH).astype(u.dtype), jnp.stack([S_f, S_b], axis=1)


def fnet_mixer(u):
    b, l, _ = u.shape
    xf = u.astype(jnp.float32).reshape(b, l, FNET_GROUPS, FNET_GROUP_DIM)
    y = jnp.fft.fft2(xf, axes=(1, 3), norm='ortho').real
    return y.reshape(b, l, FNET_WIDTH).astype(u.dtype)


def axial_rope(l):
    rows = l // GRID_W
    row = jnp.repeat(jnp.arange(rows, dtype=jnp.float32), GRID_W)
    col = jnp.tile(jnp.arange(GRID_W, dtype=jnp.float32), rows)
    nf = ATT_HEAD_DIM // 4
    inv = ROPE_THETA ** (-jnp.arange(nf, dtype=jnp.float32) / nf)
    ang = jnp.concatenate([row[:, None] * inv, col[:, None] * inv], axis=-1)
    return jnp.cos(ang), jnp.sin(ang)


def apply_rope(x, cos, sin):
    half = ATT_HEAD_DIM // 2
    xf = x.astype(jnp.float32)
    x1, x2 = xf[..., :half], xf[..., half:]
    c, s = cos[None, :, None, :], sin[None, :, None, :]
    return jnp.concatenate([x1 * c - x2 * s, x2 * c + x1 * s], axis=-1).astype(x.dtype)


def sink_softmax(s, sink):
    sk = jnp.broadcast_to(sink[None, :, :, None, None], s.shape[:-1] + (1,))
    return jax.nn.softmax(jnp.concatenate([s, sk], axis=-1), axis=-1)[..., :-1]


def ctx_attention(q, k, v, sink):
    b, lc = q.shape[:2]
    nb = lc // ABLOCK
    scale = ATT_HEAD_DIM ** -0.5
    qb = q.astype(jnp.float32).reshape(b, nb, ABLOCK, ATT_KV_HEADS, ATT_REP, ATT_HEAD_DIM)
    kf, vf = k.astype(jnp.float32), v.astype(jnp.float32)
    sk = sink.astype(jnp.float32).reshape(ATT_KV_HEADS, ATT_REP)

    def one(qblk):
        s = jnp.einsum('bqgrd,bkgd->bgrqk', qblk, kf) * scale
        return jnp.einsum('bgrqk,bkgd->bqgrd', sink_softmax(s, sk), vf)

    o = lax.map(one, jnp.moveaxis(qb, 1, 0))
    return jnp.moveaxis(o, 0, 1).reshape(b, lc, ATT_WIDTH).astype(q.dtype)


def latent_attention(q, k, v, k_ctx, v_ctx, sink):
    b, l = q.shape[:2]
    nb = l // ABLOCK
    scale = ATT_HEAD_DIM ** -0.5
    f32 = jnp.float32
    qb = q.astype(f32).reshape(b, nb, ABLOCK, ATT_KV_HEADS, ATT_REP, ATT_HEAD_DIM)
    pad = ((0, 0), (ABLOCK, ABLOCK), (0, 0), (0, 0))
    kp, vp = jnp.pad(k.astype(f32), pad), jnp.pad(v.astype(f32), pad)
    kc, vc = k_ctx.astype(f32), v_ctx.astype(f32)
    sk = sink.astype(f32).reshape(ATT_KV_HEADS, ATT_REP)
    offs_q = jnp.arange(ABLOCK)
    offs_k = jnp.arange(3 * ABLOCK)

    def one(i):
        qblk = lax.dynamic_index_in_dim(qb, i, axis=1, keepdims=False)
        kl = lax.dynamic_slice_in_dim(kp, i * ABLOCK, 3 * ABLOCK, axis=1)
        vl = lax.dynamic_slice_in_dim(vp, i * ABLOCK, 3 * ABLOCK, axis=1)
        qpos = i * ABLOCK + offs_q
        kpos = (i - 1) * ABLOCK + offs_k
        ok = ((jnp.abs(qpos[:, None] - kpos[None, :]) <= WINDOW)
              & (kpos >= 0)[None, :] & (kpos < l)[None, :])
        s_loc = jnp.where(ok, jnp.einsum('bqgrd,bkgd->bgrqk', qblk, kl) * scale, NEG_INF)
        s_ctx = jnp.einsum('bqgrd,bkgd->bgrqk', qblk, kc) * scale
        pr = sink_softmax(jnp.concatenate([s_loc, s_ctx], axis=-1), sk)
        return (jnp.einsum('bgrqk,bkgd->bqgrd', pr[..., :3 * ABLOCK], vl)
                + jnp.einsum('bgrqk,bkgd->bqgrd', pr[..., 3 * ABLOCK:], vc))

    o = lax.map(one, jnp.arange(nb))
    return jnp.moveaxis(o, 0, 1).reshape(b, l, ATT_WIDTH).astype(q.dtype)


def peer(x, w_q, sub_keys, u_tab, v_tab):
    b, l, d = x.shape
    t = b * l
    f32 = jnp.float32
    xt = x.reshape(t, d)
    q = (xt @ w_q).astype(f32).reshape(t, PEER_HEADS, 2, PEER_QDIM // 2)
    s = jnp.einsum('thpc,hpkc->thpk', q, sub_keys.astype(f32))
    s_top, i_top = lax.top_k(s, PEER_TOPK)
    n_cand = PEER_TOPK * PEER_TOPK
    cand = (s_top[:, :, 0, :, None] + s_top[:, :, 1, None, :]).reshape(t, PEER_HEADS, n_cand)
    cand_idx = (i_top[:, :, 0, :, None] * PEER_NKEYS + i_top[:, :, 1, None, :]).reshape(
        t, PEER_HEADS, n_cand)
    best, pos = lax.top_k(cand, PEER_TOPK)
    idx = jnp.take_along_axis(cand_idx, pos, axis=-1).reshape(t, PEER_HEADS * PEER_TOPK)
    gate = jax.nn.softmax(best, axis=-1).reshape(t, PEER_HEADS * PEER_TOPK)
    nblk = t // PEER_TOKEN_BLOCK

    def block(args):
        xb, ib, gb = args
        act = jax.nn.gelu(jnp.einsum('ted,td->te', jnp.take(u_tab, ib, axis=0), xb).astype(f32),
                          approximate=False)
        w = (gb * act).astype(x.dtype)
        return jnp.einsum('te,ted->td', w, jnp.take(v_tab, ib, axis=0))

    out = lax.map(block, (xt.reshape(nblk, PEER_TOKEN_BLOCK, d),
                          idx.reshape(nblk, PEER_TOKEN_BLOCK, -1),
                          gate.reshape(nblk, PEER_TOKEN_BLOCK, -1)))
    return out.reshape(b, l, d)


def trunk_layer(x, mod, p, ssd_h0, gdn_h0, kv_ctx):
    sh1, sc1, g1, sh2, sc2, g2 = mod
    b, l, _ = x.shape
    h = rmsnorm(x, p['norm1_w']) * (1.0 + sc1) + sh1
    u = h @ p['w_in']
    u_ssd, u_gdn, u_fn, u_att = jnp.split(
        u, [SSD_COLS, SSD_COLS + GDN_COLS, SSD_COLS + GDN_COLS + FNET_COLS], axis=-1)
    y_ssd, ssd_state = ssd_mixer(u_ssd, p['ssd_conv_w'], p['ssd_conv_b'], p['ssd_A_log'],
                                 p['ssd_dt_bias'], p['ssd_D'], p['ssd_norm_w'], ssd_h0)
    y_gdn, gdn_state = gdn_mixer(u_gdn, p['gdn_conv_w'], p['gdn_conv_b'], p['gdn_A_log'],
                                 p['gdn_dt_bias'], p['gdn_norm_w'], gdn_h0)
    y_fn = fnet_mixer(u_fn)
    q, k, v = jnp.split(u_att, [ATT_WIDTH, ATT_WIDTH + ATT_KV_HEADS * ATT_HEAD_DIM], axis=-1)
    q = rmsnorm(q.reshape(b, l, ATT_HEADS, ATT_HEAD_DIM), p['q_norm_w'])
    k = rmsnorm(k.reshape(b, l, ATT_KV_HEADS, ATT_HEAD_DIM), p['k_norm_w'])
    v = v.reshape(b, l, ATT_KV_HEADS, ATT_HEAD_DIM)
    if kv_ctx is None:
        y_att = ctx_attention(q, k, v, p['att_sink'])
        kv_out = (k, v)
    else:
        cos, sin = axial_rope(l)
        y_att = latent_attention(apply_rope(q, cos, sin), apply_rope(k, cos, sin), v,
                                 kv_ctx[0], kv_ctx[1], p['att_sink'])
        kv_out = kv_ctx
    y = jnp.concatenate([y_ssd, y_gdn, y_fn, y_att], axis=-1) @ p['w_out']
    x = x + g1 * y
    h2 = rmsnorm(x, p['norm2_w']) * (1.0 + sc2) + sh2
    x = x + g2 * peer(h2, p['peer_w_q'], p['peer_keys'], p['peer_u'], p['peer_v'])
    return x, ssd_state, gdn_state, kv_out


def setup_inputs(seed: int = 0) -> dict:
    key = jax.random.key(seed)
    ks = iter(jax.random.split(key, 48))
    f32 = jnp.float32

    def nrm(shape, scale):
        return jax.random.normal(next(ks), shape, f32) * scale

    def gain(shape):
        return 1.0 + nrm(shape, 0.05)

    def dt_bias_init(shape):
        dt = jnp.exp(jax.random.uniform(next(ks), shape, f32, math.log(1e-3), math.log(1e-1)))
        return dt + jnp.log(-jnp.expm1(-dt))

    def a_log_init(shape):
        return jnp.log(jax.random.uniform(next(ks), shape, f32, 1.0, 16.0))

    return {
        'x_prompt': nrm((BATCH, SEQ, D_MODEL), 1.0),
        'x_sample': nrm((DEC_BATCH, DEC_SEQ, D_MODEL), 1.0),
        'cache_k': nrm((DEC_BATCH, DEPTH, PAST_LEN, ATT_KV_HEADS, ATT_HEAD_DIM), 1.0),
        'cache_v': nrm((DEC_BATCH, DEPTH, PAST_LEN, ATT_KV_HEADS, ATT_HEAD_DIM), 1.0),
        'state_ssd': nrm((DEC_BATCH, DEPTH, 2, SSD_HEADS, SSD_HEAD_DIM, SSD_STATE), 0.1),
        'state_gdn': nrm((DEC_BATCH, DEPTH, 2, GDN_HEADS, GDN_HEAD_DIM, GDN_HEAD_DIM), 0.1),
        'c': nrm((DEC_BATCH, D_MODEL), 1.0),
        'c_ctx': nrm((D_MODEL,), 1.0),
        'w_mod': nrm((DEPTH, D_MODEL, 6 * D_MODEL), 0.5 * D_MODEL ** -0.5),
        'b_mod': nrm((DEPTH, 6 * D_MODEL), 0.02),
        'norm1_w': gain((DEPTH, D_MODEL)),
        'norm2_w': gain((DEPTH, D_MODEL)),
        'w_in': nrm((DEPTH, D_MODEL, IN_COLS), D_MODEL ** -0.5),
        'w_out': nrm((DEPTH, MIX_WIDTH, D_MODEL), MIX_WIDTH ** -0.5),
        'ssd_conv_w': nrm((DEPTH, CONV_W, SSD_CONV_CH), CONV_W ** -0.5),
        'ssd_conv_b': nrm((DEPTH, SSD_CONV_CH), 0.02),
        'ssd_A_log': a_log_init((DEPTH, 2, SSD_HEADS)),
        'ssd_dt_bias': dt_bias_init((DEPTH, 2, SSD_HEADS)),
        'ssd_D': gain((DEPTH, SSD_HEADS)),
        'ssd_norm_w': gain((DEPTH, SSD_WIDTH)),
        'gdn_conv_w': nrm((DEPTH, CONV_W, 3 * GDN_WIDTH), CONV_W ** -0.5),
        'gdn_conv_b': nrm((DEPTH, 3 * GDN_WIDTH), 0.02),
        'gdn_A_log': a_log_init((DEPTH, 2, GDN_HEADS)),
        'gdn_dt_bias': dt_bias_init((DEPTH, 2, GDN_HEADS)),
        'gdn_norm_w': gain((DEPTH, GDN_HEAD_DIM)),
        'q_norm_w': gain((DEPTH, ATT_HEAD_DIM)),
        'k_norm_w': gain((DEPTH, ATT_HEAD_DIM)),
        'att_sink': nrm((DEPTH, ATT_HEADS), 0.5),
        'peer_w_q': nrm((DEPTH, D_MODEL, PEER_HEADS * PEER_QDIM), D_MODEL ** -0.5),
        'peer_keys': nrm((DEPTH, PEER_HEADS, 2, PEER_NKEYS, PEER_QDIM // 2), (PEER_QDIM // 2) ** -0.5),
        'peer_u': nrm((DEPTH, PEER_EXPERTS, D_MODEL), D_MODEL ** -0.5),
        'peer_v': nrm((DEPTH, PEER_EXPERTS, D_MODEL), 0.5),
    }


def reference(x_prompt, x_sample, cache_k, cache_v, state_ssd, state_gdn, c, c_ctx,
              w_mod, b_mod, norm1_w, norm2_w, w_in, w_out,
              ssd_conv_w, ssd_conv_b, ssd_A_log, ssd_dt_bias, ssd_D, ssd_norm_w,
              gdn_conv_w, gdn_conv_b, gdn_A_log, gdn_dt_bias, gdn_norm_w,
              q_norm_w, k_norm_w, att_sink, peer_w_q, peer_keys, peer_u, peer_v):
    stacked = dict(w_mod=w_mod, b_mod=b_mod, norm1_w=norm1_w, norm2_w=norm2_w, w_in=w_in,
                   w_out=w_out, ssd_conv_w=ssd_conv_w, ssd_conv_b=ssd_conv_b,
                   ssd_A_log=ssd_A_log, ssd_dt_bias=ssd_dt_bias, ssd_D=ssd_D,
                   ssd_norm_w=ssd_norm_w, gdn_conv_w=gdn_conv_w, gdn_conv_b=gdn_conv_b,
                   gdn_A_log=gdn_A_log, gdn_dt_bias=gdn_dt_bias, gdn_norm_w=gdn_norm_w,
                   q_norm_w=q_norm_w, k_norm_w=k_norm_w, att_sink=att_sink,
                   peer_w_q=peer_w_q, peer_keys=peer_keys, peer_u=peer_u, peer_v=peer_v)
    bp = x_prompt.shape[0]
    zeros_ssd = jnp.zeros((bp, 2, SSD_HEADS, SSD_HEAD_DIM, SSD_STATE), jnp.float32)
    zeros_gdn = jnp.zeros((bp, 2, GDN_HEADS, GDN_HEAD_DIM, GDN_HEAD_DIM), jnp.float32)

    xp = x_prompt
    ks, vs, ssd_states, gdn_states = [], [], [], []
    for i in range(DEPTH):
        p = {name: arr[i] for name, arr in stacked.items()}
        mod = adaln(c_ctx[None, :], p['w_mod'], p['b_mod'])
        xp, s_ssd, s_gdn, kv = trunk_layer(xp, mod, p, zeros_ssd, zeros_gdn, None)
        ks.append(kv[0])
        vs.append(kv[1])
        ssd_states.append(s_ssd)
        gdn_states.append(s_gdn)
    new_cache_k = jnp.stack(ks, axis=1).astype(x_prompt.dtype)
    new_cache_v = jnp.stack(vs, axis=1).astype(x_prompt.dtype)
    new_state_ssd = jnp.stack(ssd_states, axis=1).astype(x_prompt.dtype)
    new_state_gdn = jnp.stack(gdn_states, axis=1).astype(x_prompt.dtype)

    xs = x_sample
    for i in range(DEPTH):
        p = {name: arr[i] for name, arr in stacked.items()}
        mod = adaln(c, p['w_mod'], p['b_mod'])
        xs, _, _, _ = trunk_layer(xs, mod, p, state_ssd[:, i], state_gdn[:, i],
                                  (cache_k[:, i], cache_v[:, i]))

    return (xp, xs, new_cache_k, new_cache_v, new_state_ssd, new_state_gdn)
```

```python
import functools
import math

import jax
import jax.numpy as jnp
from jax import lax
from jax.experimental import pallas as pl
from jax.experimental.pallas import tpu as pltpu

f32 = jnp.float32
bf16 = jnp.bfloat16
HI = lax.Precision.HIGHEST

D_MODEL = 1024
DEPTH = 4
EPS = 1e-6
NEG_INF = -1e30
CONV_W = 5
HEAD_DIM = 64
N_HEADS = 4
MIX = 256
SSD_CHUNK = 256
GDN_CHUNK = 64
WINDOW = 128
ABLOCK = 128
GRID_W = 64
ROPE_THETA = 10000.0
PEER_HEADS = 8
PEER_NKEYS = 128
PEER_TOPK = 16
PEER_EXPERTS = PEER_NKEYS * PEER_NKEYS

VMEM_LIMIT = 56 * 1024 * 1024

U_SSD, U_GDN, U_FN, U_ATT, U_SM = 768, 1024, 256, 512, 128
U_MAIN = U_SSD + U_GDN + U_FN + U_ATT


def _dot(a, b, prec=None):
    return jnp.dot(a, b, preferred_element_type=f32, precision=prec)


def _dot_nt(a, b, prec=None):
    return lax.dot_general(a, b, (((1,), (1,)), ((), ())), preferred_element_type=f32, precision=prec)


def _dot_tn(a, b, prec=None):
    return lax.dot_general(a, b, (((0,), (0,)), ((), ())), preferred_element_type=f32, precision=prec)


def _sigmoid(x):
    return 1.0 / (1.0 + jnp.exp(-x))


def _silu(x):
    return x * _sigmoid(x)


def _softplus(x):
    return jnp.maximum(x, 0.0) + jnp.log(1.0 + jnp.exp(-jnp.abs(x)))


def _params(*sem):
    return pltpu.CompilerParams(dimension_semantics=sem, vmem_limit_bytes=VMEM_LIMIT)


def _idiv(x, n):
    return lax.shift_right_logical(x, int(math.log2(n)))


def _head_mean_matrix(width):
    r = _idiv(lax.broadcasted_iota(jnp.int32, (width, width), 0), HEAD_DIM)
    c = _idiv(lax.broadcasted_iota(jnp.int32, (width, width), 1), HEAD_DIM)
    return jnp.where(r == c, 1.0 / HEAD_DIM, 0.0).astype(f32)


def _shift_rows(x, d, n):
    row = lax.broadcasted_iota(jnp.int32, x.shape, 0)
    y = pltpu.roll(x, (-d) % n, 0)
    ok = (row + d >= 0) & (row + d < n)
    return jnp.where(ok, y, 0.0)


def _conv_silu(x, w_ref, b_ref, n):
    half = (CONV_W - 1) // 2
    acc = b_ref[...] + w_ref[half:half + 1, :] * x
    for k in range(CONV_W):
        if k != half:
            acc = acc + w_ref[k:k + 1, :] * _shift_rows(x, k - half, n)
    return _silu(acc)


def _mod_body(c_ref, w_ref, b_ref, o_ref):
    o_ref[0] = _dot(_silu(c_ref[...]), w_ref[0], HI) + b_ref[0]


def adaln_all(cvecs, w_mod, b_mod):
    tn = 1536
    n = w_mod.shape[-1]
    return pl.pallas_call(
        _mod_body,
        grid=(DEPTH, n // tn),
        in_specs=[pl.BlockSpec((8, D_MODEL), lambda l, j: (0, 0)),
                  pl.BlockSpec((1, D_MODEL, tn), lambda l, j: (l, 0, j)),
                  pl.BlockSpec((1, 1, tn), lambda l, j: (l, 0, j))],
        out_specs=pl.BlockSpec((1, 8, tn), lambda l, j: (l, 0, j)),
        out_shape=jax.ShapeDtypeStruct((DEPTH, 8, n), f32),
        compiler_params=_params("arbitrary", "arbitrary"),
        name="adaln",
    )(cvecs, w_mod, b_mod.reshape(DEPTH, 1, n))


def _in_body(x_ref, nw_ref, sc_ref, sh_ref, w_ref, ws_ref, wst_ref,
             o_ssd, o_gdn, o_fn, o_att, o_sm, o_smt):
    x = x_ref[...]
    h = x * lax.rsqrt(jnp.mean(x * x, axis=-1, keepdims=True) + EPS) * nw_ref[...]
    h = h * (1.0 + sc_ref[0]) + sh_ref[0]
    u = _dot(h.astype(bf16), w_ref[...])
    o_ssd[...] = u[:, 0:U_SSD]
    o_gdn[...] = u[:, U_SSD:U_SSD + U_GDN]
    o_fn[...] = u[:, U_SSD + U_GDN:U_SSD + U_GDN + U_FN]
    o_att[...] = u[:, U_SSD + U_GDN + U_FN:U_MAIN]
    o_sm[...] = _dot(h, ws_ref[...], HI)
    o_smt[...] = _dot_nt(wst_ref[...], h, HI)


def _mod_row_map(tm, t_p, l_s):
    def f(i):
        t0 = i * tm
        return (jnp.where(t0 < t_p, 0, 1 + (t0 - t_p) // l_s), 0, 0)
    return f


def in_proj(x, nw, sc, sh, w_main, w_sm, w_smt, t_p, l_s):
    t = x.shape[0]
    tm = 512
    row = _mod_row_map(tm, t_p, l_s)
    widths = (U_SSD, U_GDN, U_FN, U_ATT, U_SM)
    return pl.pallas_call(
        _in_body,
        grid=(t // tm,),
        in_specs=[pl.BlockSpec((tm, D_MODEL), lambda i: (i, 0)),
                  pl.BlockSpec((1, D_MODEL), lambda i: (0, 0)),
                  pl.BlockSpec((1, 1, D_MODEL), row),
                  pl.BlockSpec((1, 1, D_MODEL), row),
                  pl.BlockSpec((D_MODEL, U_MAIN), lambda i: (0, 0)),
                  pl.BlockSpec((D_MODEL, U_SM), lambda i: (0, 0)),
                  pl.BlockSpec((U_SM, D_MODEL), lambda i: (0, 0))],
        out_specs=[pl.BlockSpec((tm, w), lambda i: (i, 0)) for w in widths]
        + [pl.BlockSpec((U_SM, tm), lambda i: (0, i))],
        out_shape=[jax.ShapeDtypeStruct((t, w), f32) for w in widths]
        + [jax.ShapeDtypeStruct((U_SM, t), f32)],
        compiler_params=_params("arbitrary"),
        name="in_proj",
    )(x, nw, sc, sh, w_main, w_sm, w_smt)


def _ssd_body(L, u_ref, sm_ref, smt_ref, cw_ref, cb_ref, alr_ref, alc_ref, dbr_ref, dbc_ref,
              dsk_ref, nw_ref, h0_ref, y_ref, hT_ref, yacc):
    Q = min(L, SSD_CHUNK)
    nc = L // Q
    hd = HEAD_DIM
    z = u_ref[:, 0:MIX]
    xbc = _conv_silu(u_ref[:, MIX:U_SSD], cw_ref, cb_ref, L)
    xs = xbc[:, 0:MIX]
    dtc = _softplus(sm_ref[:, 0:8] + dbr_ref[...])
    dtr = _softplus(smt_ref[0:8, :] + dbc_ref[...])
    ac = dtc * (-jnp.exp(alr_ref[...]))
    ar = dtr * (-jnp.exp(alc_ref[...]))
    ri = lax.broadcasted_iota(jnp.int32, (Q, Q), 0)
    ci = lax.broadcasted_iota(jnp.int32, (Q, Q), 1)
    low = ri >= ci
    upp = ci >= ri
    tril = jnp.where(low, 1.0, 0.0).astype(f32)

    def chunk_terms(c):
        r0 = c * Q
        a_c = ac[r0:r0 + Q, :]
        a_r = ar[:, r0:r0 + Q]
        pc = _dot(tril, a_c, HI)
        pr = _dot_nt(a_r, tril, HI)
        return a_c, a_r, pc, pr

    hf = [h0_ref[0, 0, h] for h in range(N_HEADS)]
    for c in range(nc):
        r0 = c * Q
        a_c, a_r, pc, pr = chunk_terms(c)
        ys = []
        for h in range(N_HEADS):
            g = h // 2
            x_h = xs[r0:r0 + Q, h * hd:(h + 1) * hd]
            b_g = xbc[r0:r0 + Q, MIX + g * hd:MIX + (g + 1) * hd]
            c_g = xbc[r0:r0 + Q, MIX + 2 * hd + g * hd:MIX + 2 * hd + (g + 1) * hd]
            gm = _dot_nt(c_g, b_g, HI)
            hb = N_HEADS + h
            seg_f = pc[:, h:h + 1] - pr[h:h + 1, :]
            l_f = jnp.where(low, jnp.exp(jnp.minimum(seg_f, 0.0)), 0.0)
            e_c = pc[:, hb:hb + 1] - a_c[:, hb:hb + 1]
            e_r = pr[hb:hb + 1, :] - a_r[hb:hb + 1, :]
            seg_b = e_r - e_c
            l_b = jnp.where(upp, jnp.exp(jnp.minimum(seg_b, 0.0)), 0.0)
            m = gm * (l_f * dtr[h:h + 1, r0:r0 + Q] + l_b * dtr[hb:hb + 1, r0:r0 + Q])
            y_h = _dot(m, x_h, HI)
            y_h = y_h + _dot_nt(c_g * jnp.exp(pc[:, h:h + 1]), hf[h], HI)
            tot = pc[Q - 1:Q, h:h + 1]
            wgt = dtc[r0:r0 + Q, h:h + 1] * jnp.exp(tot - pc[:, h:h + 1])
            hf[h] = hf[h] * jnp.exp(tot) + _dot_tn(x_h * wgt, b_g, HI)
            ys.append(y_h)
        yacc[r0:r0 + Q, :] = jnp.concatenate(ys, axis=1)
    hb_s = [h0_ref[0, 1, h] for h in range(N_HEADS)]
    for c in range(nc - 1, -1, -1):
        r0 = c * Q
        a_c, a_r, pc, pr = chunk_terms(c)
        ys = []
        for h in range(N_HEADS):
            g = h // 2
            hb = N_HEADS + h
            x_h = xs[r0:r0 + Q, h * hd:(h + 1) * hd]
            b_g = xbc[r0:r0 + Q, MIX + g * hd:MIX + (g + 1) * hd]
            c_g = xbc[r0:r0 + Q, MIX + 2 * hd + g * hd:MIX + 2 * hd + (g + 1) * hd]
            e_c = pc[:, hb:hb + 1] - a_c[:, hb:hb + 1]
            tot = pc[Q - 1:Q, hb:hb + 1]
            ys.append(_dot_nt(c_g * jnp.exp(tot - e_c), hb_s[h], HI))
            wgt = dtc[r0:r0 + Q, hb:hb + 1] * jnp.exp(e_c)
            hb_s[h] = hb_s[h] * jnp.exp(tot) + _dot_tn(x_h * wgt, b_g, HI)
        yacc[r0:r0 + Q, :] = yacc[r0:r0 + Q, :] + jnp.concatenate(ys, axis=1)
    for h in range(N_HEADS):
        hT_ref[0, 0, h] = hf[h]
        hT_ref[0, 1, h] = hb_s[h]
    y = yacc[...] + dsk_ref[...] * xs
    y = y * _silu(z)
    ms = _dot(y * y, _head_mean_matrix(MIX), HI)
    y_ref[...] = (y * lax.rsqrt(ms + EPS) * nw_ref[...]).astype(y_ref.dtype)


def ssd_mixer(u_ssd, u_sm, u_smt, t0, nseq, L, p, h0):
    b0 = t0 // L
    small = lambda s: pl.BlockSpec(s, lambda b: (0,) * len(s))
    return pl.pallas_call(
        functools.partial(_ssd_body, L),
        grid=(nseq,),
        in_specs=[pl.BlockSpec((L, U_SSD), lambda b: (b0 + b, 0)),
                  pl.BlockSpec((L, U_SM), lambda b: (b0 + b, 0)),
                  pl.BlockSpec((U_SM, L), lambda b: (0, b0 + b)),
                  small((CONV_W, 512)), small((1, 512)),
                  small((1, 8)), small((8, 1)), small((1, 8)), small((8, 1)),
                  small((1, MIX)), small((1, MIX)),
                  pl.BlockSpec((1, 2, N_HEADS, HEAD_DIM, HEAD_DIM), lambda b: (b, 0, 0, 0, 0))],
        out_specs=[pl.BlockSpec((L, MIX), lambda b: (b, 0)),
                   pl.BlockSpec((1, 2, N_HEADS, HEAD_DIM, HEAD_DIM), lambda b: (b, 0, 0, 0, 0))],
        out_shape=[jax.ShapeDtypeStruct((nseq * L, MIX), bf16),
                   jax.ShapeDtypeStruct((nseq, 2, N_HEADS, HEAD_DIM, HEAD_DIM), f32)],
        scratch_shapes=[pltpu.VMEM((L, MIX), f32)],
        compiler_params=_params("arbitrary"),
        name=f"ssd_{L}",
    )(u_ssd, u_sm, u_smt, p["conv_w"], p["conv_b"], p["al_row"], p["al_col"], p["db_row"], p["db_col"],
      p["d_skip"], p["norm_w"], h0)


def _gdn_body(L, u_ref, sm_ref, smt_ref, cw_ref, cb_ref, alr_ref, alc_ref, dbr_ref, dbc_ref,
              nw_ref, s0_ref, y_ref, sT_ref, qkv, oacc):
    Q = GDN_CHUNK
    nc = L // Q
    W = MIX
    qkv[...] = _conv_silu(u_ref[:, 0:3 * MIX], cw_ref, cb_ref, L)
    hm = _head_mean_matrix(MIX) * float(HEAD_DIM)
    q_all = qkv[:, 0:MIX]
    k_all = qkv[:, MIX:2 * MIX]
    qkv[:, 0:MIX] = q_all * lax.rsqrt(_dot(q_all * q_all, hm, HI) + EPS) * (HEAD_DIM ** -0.5)
    qkv[:, MIX:2 * MIX] = k_all * lax.rsqrt(_dot(k_all * k_all, hm, HI) + EPS)
    lac = -jnp.exp(alr_ref[...]) * _softplus(sm_ref[:, 8:16] + dbr_ref[...])
    lar = -jnp.exp(alc_ref[...]) * _softplus(smt_ref[8:16, :] + dbc_ref[...])
    btc = _sigmoid(sm_ref[:, 16:24])
    btr = _sigmoid(smt_ref[16:24, :])

    ri = lax.broadcasted_iota(jnp.int32, (W, W), 0)
    ci = lax.broadcasted_iota(jnp.int32, (W, W), 1)
    same_head = _idiv(ri, Q) == _idiv(ci, Q)
    blk16 = _idiv(ri, 16) == _idiv(ci, 16)
    eye = jnp.where(ri == ci, 1.0, 0.0).astype(f32)
    rq = lax.broadcasted_iota(jnp.int32, (Q, Q), 0)
    cq = lax.broadcasted_iota(jnp.int32, (Q, Q), 1)
    tril_q = jnp.where(rq >= cq, 1.0, 0.0).astype(f32)
    triu_q = jnp.where(cq >= rq, 1.0, 0.0).astype(f32)

    def stack_cols(m, base):
        return jnp.concatenate([m[:, base + h:base + h + 1] for h in range(N_HEADS)], axis=0)

    def stack_rows(m, base):
        return jnp.concatenate([m[base + h:base + h + 1, :] for h in range(N_HEADS)], axis=1)

    def block_diag(m):
        return jnp.where(same_head, jnp.concatenate([m] * N_HEADS, axis=1), 0.0)

    def heads_to_rows(m):
        return jnp.concatenate([m[:, h * HEAD_DIM:(h + 1) * HEAD_DIM] for h in range(N_HEADS)], axis=0)

    def rows_to_heads(m):
        return jnp.concatenate([m[h * Q:(h + 1) * Q, :] for h in range(N_HEADS)], axis=1)

    def sweep(d, S):
        tri_mat = tril_q if d == 0 else triu_q
        causal = (ri >= ci) if d == 0 else (ci >= ri)
        strict = (ri > ci) if d == 0 else (ci > ri)
        order = range(nc) if d == 0 else range(nc - 1, -1, -1)
        for c in order:
            r0 = c * Q
            qs = heads_to_rows(qkv[r0:r0 + Q, 0:MIX])
            ks = heads_to_rows(qkv[r0:r0 + Q, MIX:2 * MIX])
            vs = heads_to_rows(qkv[r0:r0 + Q, 2 * MIX:3 * MIX])
            la_c = lac[r0:r0 + Q, :]
            la_r = lar[:, r0:r0 + Q]
            gc = stack_cols(_dot(tri_mat, la_c, HI), 4 * d)
            gr = stack_rows(_dot_nt(la_r, tri_mat, HI), 4 * d)
            bc = stack_cols(btc[r0:r0 + Q, :], 4 * d)
            ends = []
            for h in range(N_HEADS):
                e = gr[:, h * Q + Q - 1:h * Q + Q] if d == 0 else gr[:, h * Q:h * Q + 1]
                ends.append(jnp.broadcast_to(e, (Q, 1)))
            g_end = jnp.concatenate(ends, axis=0)
            decay = jnp.where(causal & same_head, jnp.exp(jnp.minimum(gc - gr, 0.0)), 0.0)
            kbd = block_diag(ks)
            kk = _dot_nt(kbd, kbd, HI)
            a = jnp.where(strict, bc * kk * decay, 0.0)
            n = jnp.where(blk16, a, 0.0)
            n2 = _dot(n, n, HI)
            n4 = _dot(n2, n2, HI)
            n8 = _dot(n4, n4, HI)
            dinv = _dot(_dot(eye - n, eye + n2, HI), _dot(eye + n4, eye + n8, HI), HI)
            zz = _dot(dinv, a - n, HI)
            z2 = _dot(zz, zz, HI)
            tinv = _dot(_dot(eye - zz, eye + z2, HI), dinv, HI)
            rhs = jnp.concatenate([vs * bc, ks * (bc * jnp.exp(gc))], axis=1)
            sol = _dot(tinv, rhs, HI)
            u_c = sol[:, 0:HEAD_DIM]
            w_c = sol[:, HEAD_DIM:2 * HEAD_DIM]
            attn = jnp.where(causal, _dot_nt(block_diag(qs), kbd, HI) * decay, 0.0)
            v_new = u_c - _dot(block_diag(w_c), S, HI)
            o = _dot(block_diag(qs * jnp.exp(gc)), S, HI) + _dot(attn, v_new, HI)
            S = S * jnp.exp(g_end) + _dot_tn(block_diag(ks * jnp.exp(g_end - gc)), v_new, HI)
            o = rows_to_heads(o)
            if d == 0:
                oacc[r0:r0 + Q, :] = o
            else:
                oacc[r0:r0 + Q, :] = oacc[r0:r0 + Q, :] + o
        return S

    for d in range(2):
        S0 = jnp.concatenate([s0_ref[0, d, h] for h in range(N_HEADS)], axis=0)
        S = sweep(d, S0)
        for h in range(N_HEADS):
            sT_ref[0, d, h] = S[h * HEAD_DIM:(h + 1) * HEAD_DIM, :]
    o = oacc[...]
    ms = _dot(o * o, _head_mean_matrix(MIX), HI)
    o = o * lax.rsqrt(ms + EPS) * nw_ref[...]
    y_ref[...] = (o * _silu(u_ref[:, 3 * MIX:4 * MIX])).astype(y_ref.dtype)


def gdn_mixer(u_gdn, u_sm, u_smt, t0, nseq, L, p, s0):
    b0 = t0 // L
    small = lambda s: pl.BlockSpec(s, lambda b: (0,) * len(s))
    return pl.pallas_call(
        functools.partial(_gdn_body, L),
        grid=(nseq,),
        in_specs=[pl.BlockSpec((L, U_GDN), lambda b: (b0 + b, 0)),
                  pl.BlockSpec((L, U_SM), lambda b: (b0 + b, 0)),
                  pl.BlockSpec((U_SM, L), lambda b: (0, b0 + b)),
                  small((CONV_W, 3 * MIX)), small((1, 3 * MIX)),
                  small((1, 8)), small((8, 1)), small((1, 8)), small((8, 1)),
                  small((1, MIX)),
                  pl.BlockSpec((1, 2, N_HEADS, HEAD_DIM, HEAD_DIM), lambda b: (b, 0, 0, 0, 0))],
        out_specs=[pl.BlockSpec((L, MIX), lambda b: (b, 0)),
                   pl.BlockSpec((1, 2, N_HEADS, HEAD_DIM, HEAD_DIM), lambda b: (b, 0, 0, 0, 0))],
        out_shape=[jax.ShapeDtypeStruct((nseq * L, MIX), bf16),
                   jax.ShapeDtypeStruct((nseq, 2, N_HEADS, HEAD_DIM, HEAD_DIM), f32)],
        scratch_shapes=[pltpu.VMEM((L, 3 * MIX), f32), pltpu.VMEM((L, MIX), f32)],
        compiler_params=_params("arbitrary"),
        name=f"gdn_{L}",
    )(u_gdn, u_sm, u_smt, p["conv_w"], p["conv_b"], p["al_row"], p["al_col"], p["db_row"], p["db_col"],
      p["norm_w"], s0)


def _fnet_body(x_ref, cl_ref, sl_ref, cc_ref, sc_ref, y_ref):
    x = x_ref[...]
    xc = _dot(x, cc_ref[...], HI)
    xs = _dot(x, sc_ref[...], HI)
    y_ref[...] = (_dot(cl_ref[...], xc, HI) - _dot(sl_ref[...], xs, HI)).astype(y_ref.dtype)


def fnet_mixer(u_fn, t0, nseq, L, tabs):
    b0 = t0 // L
    full = lambda s: pl.BlockSpec(s, lambda b: (0,) * len(s))
    return pl.pallas_call(
        _fnet_body,
        grid=(nseq,),
        in_specs=[pl.BlockSpec((L, MIX), lambda b: (b0 + b, 0)),
                  full((L, L)), full((L, L)), full((MIX, MIX)), full((MIX, MIX))],
        out_specs=pl.BlockSpec((L, MIX), lambda b: (b, 0)),
        out_shape=jax.ShapeDtypeStruct((nseq * L, MIX), bf16),
        compiler_params=_params("arbitrary"),
        name=f"fnet_{L}",
    )(u_fn, *tabs)


def dft_tables(L):
    n = jnp.arange(L, dtype=jnp.int32)
    ang_l = (2.0 * math.pi / L) * ((n[:, None] * n[None, :]) % L).astype(f32)
    m = jnp.arange(MIX, dtype=jnp.int32)
    same = (m[:, None] // HEAD_DIM) == (m[None, :] // HEAD_DIM)
    ang_c = (2.0 * math.pi / HEAD_DIM) * (((m[:, None] % HEAD_DIM) * (m[None, :] % HEAD_DIM)) % HEAD_DIM).astype(f32)
    sl = 1.0 / math.sqrt(L)
    sc = 1.0 / math.sqrt(HEAD_DIM)
    return (jnp.cos(ang_l) * sl, jnp.sin(ang_l) * sl,
            jnp.where(same, jnp.cos(ang_c) * sc, 0.0), jnp.where(same, jnp.sin(ang_c) * sc, 0.0))


def _qk_norm(x, w_row, width):
    ms = _dot(x * x, _head_mean_matrix(width), HI)
    return x * lax.rsqrt(ms + EPS) * w_row


def _sink_softmax_pv(s_list, v_list, sink):
    m = sink
    for s in s_list:
        m = jnp.maximum(m, jnp.max(s, axis=-1, keepdims=True))
    den = jnp.exp(sink - m)
    acc = None
    for s, v in zip(s_list, v_list):
        e = jnp.exp(s - m)
        den = den + jnp.sum(e, axis=-1, keepdims=True)
        pv = _dot(e, v, HI)
        acc = pv if acc is None else acc + pv
    return acc / den


def _ctx_att_body(L, u_ref, qw_ref, kw_ref, sink_ref, y_ref, k_ref, v_ref):
    hd = HEAD_DIM
    scale = hd ** -0.5
    q = _qk_norm(u_ref[:, 0:MIX], qw_ref[...], MIX)
    k = _qk_norm(u_ref[:, MIX:MIX + 2 * hd], kw_ref[...], 2 * hd)
    v = u_ref[:, MIX + 2 * hd:MIX + 4 * hd]
    k_ref[...] = k
    v_ref[...] = v
    outs = []
    for g in range(2):
        k_g = k[:, g * hd:(g + 1) * hd]
        v_g = v[:, g * hd:(g + 1) * hd]
        for r in range(2):
            h = 2 * g + r
            s = _dot_nt(q[:, h * hd:(h + 1) * hd], k_g, HI) * scale
            outs.append(_sink_softmax_pv([s], [v_g], sink_ref[0:1, h:h + 1]))
    y_ref[...] = jnp.concatenate(outs, axis=1).astype(y_ref.dtype)


def ctx_attention(u_att, t0, nseq, L, p):
    b0 = t0 // L
    small = lambda s: pl.BlockSpec(s, lambda b: (0,) * len(s))
    return pl.pallas_call(
        functools.partial(_ctx_att_body, L),
        grid=(nseq,),
        in_specs=[pl.BlockSpec((L, U_ATT), lambda b: (b0 + b, 0)),
                  small((1, MIX)), small((1, 2 * HEAD_DIM)), small((1, N_HEADS))],
        out_specs=[pl.BlockSpec((L, MIX), lambda b: (b, 0)),
                   pl.BlockSpec((L, 2 * HEAD_DIM), lambda b: (b, 0)),
                   pl.BlockSpec((L, 2 * HEAD_DIM), lambda b: (b, 0))],
        out_shape=[jax.ShapeDtypeStruct((nseq * L, MIX), bf16),
                   jax.ShapeDtypeStruct((nseq * L, 2 * HEAD_DIM), f32),
                   jax.ShapeDtypeStruct((nseq * L, 2 * HEAD_DIM), f32)],
        compiler_params=_params("arbitrary"),
        name="ctx_att",
    )(u_att, p["q_norm"], p["k_norm"], p["sink"])


def _rope(x, cos_ref, sin_ref, width):
    lane = lax.broadcasted_iota(jnp.int32, x.shape, 1) & (HEAD_DIM - 1)
    half = HEAD_DIM // 2
    swapped = jnp.where(lane < half, pltpu.roll(x, width - half, 1), pltpu.roll(x, half, 1))
    return x * cos_ref[...] + swapped * sin_ref[...]


def _lat_att_body(L, P, u_ref, kc_ref, vc_ref, qw_ref, kw_ref, sink_ref, cq_ref, sq_ref, ck_ref, sk_ref, y_ref):
    hd = HEAD_DIM
    scale = hd ** -0.5
    nb = L // ABLOCK
    q = _rope(_qk_norm(u_ref[:, 0:MIX], qw_ref[...], MIX), cq_ref, sq_ref, MIX)
    k = _rope(_qk_norm(u_ref[:, MIX:MIX + 2 * hd], kw_ref[...], 2 * hd), ck_ref, sk_ref, 2 * hd)
    v = u_ref[:, MIX + 2 * hd:MIX + 4 * hd]
    kc = kc_ref[0]
    vc = vc_ref[0]
    for i in range(nb):
        lo = max(i - 1, 0) * ABLOCK
        hi = min(i + 2, nb) * ABLOCK
        qpos = i * ABLOCK + lax.broadcasted_iota(jnp.int32, (ABLOCK, hi - lo), 0)
        kpos = lo + lax.broadcasted_iota(jnp.int32, (ABLOCK, hi - lo), 1)
        dist = qpos - kpos
        ok = (dist <= WINDOW) & (dist >= -WINDOW)
        outs = []
        for g in range(2):
            k_l = k[lo:hi, g * hd:(g + 1) * hd]
            v_l = v[lo:hi, g * hd:(g + 1) * hd]
            k_c = kc[:, g * hd:(g + 1) * hd]
            v_c = vc[:, g * hd:(g + 1) * hd]
            for r in range(2):
                h = 2 * g + r
                q_h = q[i * ABLOCK:(i + 1) * ABLOCK, h * hd:(h + 1) * hd]
                s_loc = jnp.where(ok, _dot_nt(q_h, k_l, HI) * scale, NEG_INF)
                s_ctx = _dot_nt(q_h, k_c, HI) * scale
                outs.append(_sink_softmax_pv([s_loc, s_ctx], [v_l, v_c], sink_ref[0:1, h:h + 1]))
        y_ref[i * ABLOCK:(i + 1) * ABLOCK, :] = jnp.concatenate(outs, axis=1).astype(y_ref.dtype)


def latent_attention(u_att, t0, nseq, L, p, kc, vc, rope):
    b0 = t0 // L
    P = kc.shape[1]
    small = lambda s: pl.BlockSpec(s, lambda b: (0,) * len(s))
    return pl.pallas_call(
        functools.partial(_lat_att_body, L, P),
        grid=(nseq,),
        in_specs=[pl.BlockSpec((L, U_ATT), lambda b: (b0 + b, 0)),
                  pl.BlockSpec((1, P, 2 * HEAD_DIM), lambda b: (b, 0, 0)),
                  pl.BlockSpec((1, P, 2 * HEAD_DIM), lambda b: (b, 0, 0)),
                  small((1, MIX)), small((1, 2 * HEAD_DIM)), small((1, N_HEADS)),
                  small((L, MIX)), small((L, MIX)), small((L, 2 * HEAD_DIM)), small((L, 2 * HEAD_DIM))],
        out_specs=pl.BlockSpec((L, MIX), lambda b: (b, 0)),
        out_shape=jax.ShapeDtypeStruct((nseq * L, MIX), bf16),
        compiler_params=_params("arbitrary"),
        name="lat_att",
    )(u_att, kc, vc, p["q_norm"], p["k_norm"], p["sink"], *rope)


def rope_tables(L):
    rows = L // GRID_W
    row = jnp.repeat(jnp.arange(rows, dtype=f32), GRID_W)
    col = jnp.tile(jnp.arange(GRID_W, dtype=f32), rows)
    nf = HEAD_DIM // 4
    inv = ROPE_THETA ** (-jnp.arange(nf, dtype=f32) / nf)
    ang = jnp.concatenate([row[:, None] * inv, col[:, None] * inv], axis=-1)
    cos = jnp.concatenate([jnp.cos(ang), jnp.cos(ang)], axis=-1)
    sin = jnp.concatenate([-jnp.sin(ang), jnp.sin(ang)], axis=-1)
    return (jnp.tile(cos, (1, N_HEADS)), jnp.tile(sin, (1, N_HEADS)), jnp.tile(cos, (1, 2)), jnp.tile(sin, (1, 2)))


def _out_body(ya, yb, yc, yd, x_ref, w_ref, g1_ref, nw_ref, sc_ref, sh_ref, x1_ref, h2_ref, h2t_ref):
    y = _dot(ya[...], w_ref[0:MIX, :])
    y = y + _dot(yb[...], w_ref[MIX:2 * MIX, :])
    y = y + _dot(yc[...], w_ref[2 * MIX:3 * MIX, :])
    y = y + _dot(yd[...], w_ref[3 * MIX:4 * MIX, :])
    x1 = x_ref[...] + g1_ref[0] * y
    x1_ref[...] = x1
    h = x1 * lax.rsqrt(jnp.mean(x1 * x1, axis=-1, keepdims=True) + EPS) * nw_ref[...]
    h = h * (1.0 + sc_ref[0]) + sh_ref[0]
    h2_ref[...] = h.astype(bf16)
    h2t_ref[...] = h.T.astype(bf16)


def out_proj(ys, x, w_out, g1, nw, sc, sh, t_p, l_s):
    t = x.shape[0]
    tm = 512
    row = _mod_row_map(tm, t_p, l_s)
    tok = lambda w: pl.BlockSpec((tm, w), lambda i: (i, 0))
    return pl.pallas_call(
        _out_body,
        grid=(t // tm,),
        in_specs=[tok(MIX), tok(MIX), tok(MIX), tok(MIX), tok(D_MODEL),
                  pl.BlockSpec((D_MODEL, D_MODEL), lambda i: (0, 0)),
                  pl.BlockSpec((1, 1, D_MODEL), row),
                  pl.BlockSpec((1, D_MODEL), lambda i: (0, 0)),
                  pl.BlockSpec((1, 1, D_MODEL), row),
                  pl.BlockSpec((1, 1, D_MODEL), row)],
        out_specs=[tok(D_MODEL), tok(D_MODEL), pl.BlockSpec((D_MODEL, tm), lambda i: (0, i))],
        out_shape=[jax.ShapeDtypeStruct((t, D_MODEL), f32),
                   jax.ShapeDtypeStruct((t, D_MODEL), bf16),
                   jax.ShapeDtypeStruct((D_MODEL, t), bf16)],
        compiler_params=_params("arbitrary"),
        name="out_proj",
    )(*ys, x, w_out, g1, nw, sc, sh)


def _top16(s, n):
    io = lax.broadcasted_iota(jnp.int32, s.shape, 0)
    rank = jnp.full(s.shape, PEER_TOPK, jnp.int32)
    vals = []
    for r in range(PEER_TOPK):
        m = jnp.max(s, axis=0, keepdims=True)
        idx = jnp.min(jnp.where(s == m, io, n), axis=0, keepdims=True)
        hit = io == idx
        rank = jnp.where(hit, r, rank)
        s = jnp.where(hit, -jnp.inf, s)
        vals.append(m)
    return jnp.concatenate(vals, axis=0), rank


def _peer_route_body(h2_ref, wq_ref, keys_ref, r2_ref, e2_ref, lim_ref, cw_ref):
    k = PEER_TOPK
    q = _dot(h2_ref[...], wq_ref[...])
    tn = q.shape[0]
    ra = lax.broadcasted_iota(jnp.int32, (k, k * k), 0)
    rc = lax.broadcasted_iota(jnp.int32, (k, k * k), 1)
    rep_a = jnp.where(_idiv(rc, k) == ra, 1.0, 0.0).astype(f32)
    for h in range(PEER_HEADS):
        s1 = _dot_nt(keys_ref[h, 0], q[:, h * 256:h * 256 + 128], HI)
        s2 = _dot_nt(keys_ref[h, 1], q[:, h * 256 + 128:h * 256 + 256], HI)
        t1, rank1 = _top16(s1, PEER_NKEYS)
        t2, rank2 = _top16(s2, PEER_NKEYS)
        cand = jnp.concatenate([t1[a:a + 1, :] + t2 for a in range(k)], axis=0)
        best, crank = _top16(cand, k * k)
        sel = jnp.where(crank < k, 1.0, 0.0).astype(f32)
        count_a = _dot(rep_a, sel, HI)
        zsum = jnp.sum(jnp.exp(best - best[0:1, :]), axis=0, keepdims=True)
        lim = jnp.zeros((PEER_NKEYS, tn), f32)
        for a in range(k):
            lim = jnp.where(rank1 == a, count_a[a:a + 1, :], lim)
        r2_ref[h] = rank2.astype(f32)
        e2_ref[h] = jnp.exp(s2 - t2[0:1, :])
        lim_ref[h] = lim
        cw_ref[h] = jnp.exp(s1 - t1[0:1, :]) / zsum


def peer_route(h2, w_q, keys):
    t = h2.shape[0]
    tn = 256
    arr = jax.ShapeDtypeStruct((PEER_HEADS, PEER_NKEYS, t), f32)
    spec = pl.BlockSpec((PEER_HEADS, PEER_NKEYS, tn), lambda i: (0, 0, i))
    return pl.pallas_call(
        _peer_route_body,
        grid=(t // tn,),
        in_specs=[pl.BlockSpec((tn, D_MODEL), lambda i: (i, 0)),
                  pl.BlockSpec((D_MODEL, 2 * D_MODEL), lambda i: (0, 0)),
                  pl.BlockSpec((PEER_HEADS, 2, PEER_NKEYS, PEER_NKEYS), lambda i: (0, 0, 0, 0))],
        out_specs=[spec, spec, spec, spec],
        out_shape=[arr, arr, arr, arr],
        compiler_params=_params("arbitrary"),
        name="peer_route",
    )(h2, w_q, keys)


def _peer_expert_body(eb_i, h2t_ref, u_ref, v_ref, r2_ref, e2_ref, lim_ref, cw_ref, x1_ref, g2_ref,
                      o_ref, wt, acc):
    e = pl.program_id(1)

    @pl.when(e == 0)
    def _():
        acc[...] = jnp.zeros_like(acc)

    st = _dot(u_ref[...], h2t_ref[...])
    for ii in range(eb_i):
        i = e * eb_i + ii
        g = None
        for h in range(PEER_HEADS):
            lim = lim_ref[h, pl.ds(i, 1), :]
            cw = cw_ref[h, pl.ds(i, 1), :]
            term = jnp.where(r2_ref[h] < lim, e2_ref[h] * cw, 0.0)
            g = term if g is None else g + term
        s = st[ii * PEER_NKEYS:(ii + 1) * PEER_NKEYS, :]
        act = 0.5 * s * (1.0 + lax.erf(s * (2.0 ** -0.5)))
        wt[ii * PEER_NKEYS:(ii + 1) * PEER_NKEYS, :] = (g * act).astype(bf16)
    acc[...] += _dot_tn(wt[...], v_ref[...])

    @pl.when(e == pl.num_programs(1) - 1)
    def _():
        o_ref[...] = x1_ref[...] + g2_ref[0] * acc[...]


def peer_experts(h2t, u_tab, v_tab, route, x1, g2, t_p, l_s):
    t = x1.shape[0]
    tn = 512
    eb_i = 4
    eb = eb_i * PEER_NKEYS
    row = _mod_row_map(tn, t_p, l_s)
    rspec = pl.BlockSpec((PEER_HEADS, PEER_NKEYS, tn), lambda j, e: (0, 0, j))
    return pl.pallas_call(
        functools.partial(_peer_expert_body, eb_i),
        grid=(t // tn, PEER_EXPERTS // eb),
        in_specs=[pl.BlockSpec((D_MODEL, tn), lambda j, e: (0, j)),
                  pl.BlockSpec((eb, D_MODEL), lambda j, e: (e, 0)),
                  pl.BlockSpec((eb, D_MODEL), lambda j, e: (e, 0)),
                  rspec, rspec, rspec, rspec,
                  pl.BlockSpec((tn, D_MODEL), lambda j, e: (j, 0)),
                  pl.BlockSpec((1, 1, D_MODEL), lambda j, e: row(j))],
        out_specs=pl.BlockSpec((tn, D_MODEL), lambda j, e: (j, 0)),
        out_shape=jax.ShapeDtypeStruct((t, D_MODEL), f32),
        scratch_shapes=[pltpu.VMEM((eb, tn), bf16), pltpu.VMEM((tn, D_MODEL), f32)],
        compiler_params=_params("arbitrary", "arbitrary"),
        name="peer_experts",
    )(h2t, u_tab, v_tab, *route, x1, g2)


def _regroup_w_in(w):
    o_gdn = 776
    o_fn = o_gdn + 1040
    o_att = o_fn + 256
    main = jnp.concatenate([w[:, 0:768], w[:, o_gdn:o_gdn + 1024], w[:, o_fn:o_fn + 256], w[:, o_att:o_att + 512]], axis=1)
    small = jnp.concatenate([w[:, 768:776], w[:, o_gdn + 1024:o_gdn + 1040],
                             jnp.zeros((w.shape[0], U_SM - 24), w.dtype)], axis=1)
    return main.astype(bf16), small, small.T


def _layer_params(i, ssd_conv_w, ssd_conv_b, ssd_A_log, ssd_dt_bias, ssd_D, ssd_norm_w,
                  gdn_conv_w, gdn_conv_b, gdn_A_log, gdn_dt_bias, gdn_norm_w, q_norm_w, k_norm_w, att_sink):
    ssd = dict(conv_w=ssd_conv_w[i], conv_b=ssd_conv_b[i][None, :],
               al_row=ssd_A_log[i].reshape(1, 8), al_col=ssd_A_log[i].reshape(8, 1),
               db_row=ssd_dt_bias[i].reshape(1, 8), db_col=ssd_dt_bias[i].reshape(8, 1),
               d_skip=jnp.repeat(ssd_D[i], HEAD_DIM)[None, :], norm_w=ssd_norm_w[i][None, :])
    gdn = dict(conv_w=gdn_conv_w[i], conv_b=gdn_conv_b[i][None, :],
               al_row=gdn_A_log[i].reshape(1, 8), al_col=gdn_A_log[i].reshape(8, 1),
               db_row=gdn_dt_bias[i].reshape(1, 8), db_col=gdn_dt_bias[i].reshape(8, 1),
               norm_w=jnp.tile(gdn_norm_w[i], N_HEADS)[None, :])
    att = dict(q_norm=jnp.tile(q_norm_w[i], N_HEADS)[None, :], k_norm=jnp.tile(k_norm_w[i], 2)[None, :],
               sink=att_sink[i][None, :])
    return ssd, gdn, att


def kernel(x_prompt, x_sample, cache_k, cache_v, state_ssd, state_gdn, c, c_ctx, w_mod, b_mod, norm1_w, norm2_w, w_in, w_out, ssd_conv_w, ssd_conv_b, ssd_A_log, ssd_dt_bias, ssd_D, ssd_norm_w, gdn_conv_w, gdn_conv_b, gdn_A_log, gdn_dt_bias, gdn_norm_w, q_norm_w, k_norm_w, att_sink, peer_w_q, peer_keys, peer_u, peer_v):
    nb_p, l_p, d = x_prompt.shape
    nb_s, l_s, _ = x_sample.shape
    past = cache_k.shape[2]
    t_p = nb_p * l_p
    x = jnp.concatenate([x_prompt.reshape(t_p, d), x_sample.reshape(nb_s * l_s, d)], axis=0)

    cvecs = jnp.concatenate([c_ctx[None, :], c, jnp.zeros((8 - 1 - nb_s, d), f32)], axis=0)
    mods = adaln_all(cvecs, w_mod, b_mod)
    dft_p, dft_s = dft_tables(l_p), dft_tables(l_s)
    rope = rope_tables(l_s)
    zeros_state = jnp.zeros((nb_p, 2, N_HEADS, HEAD_DIM, HEAD_DIM), f32)

    ks, vs, ssd_states, gdn_states = [], [], [], []
    for i in range(DEPTH):
        sh1, sc1, g1, sh2, sc2, g2 = [mods[i, :, j * d:(j + 1) * d].reshape(8, 1, d) for j in range(6)]
        ssd_p, gdn_p, att_p = _layer_params(i, ssd_conv_w, ssd_conv_b, ssd_A_log, ssd_dt_bias, ssd_D, ssd_norm_w,
                                            gdn_conv_w, gdn_conv_b, gdn_A_log, gdn_dt_bias, gdn_norm_w,
                                            q_norm_w, k_norm_w, att_sink)
        w_main, w_sm, w_smt = _regroup_w_in(w_in[i])
        u_ssd, u_gdn, u_fn, u_att, u_sm, u_smt = in_proj(x, norm1_w[i][None, :], sc1, sh1, w_main, w_sm, w_smt, t_p, l_s)

        y_ssd_p, st_ssd = ssd_mixer(u_ssd, u_sm, u_smt, 0, nb_p, l_p, ssd_p, zeros_state)
        y_ssd_s, _ = ssd_mixer(u_ssd, u_sm, u_smt, t_p, nb_s, l_s, ssd_p, state_ssd[:, i])
        y_gdn_p, st_gdn = gdn_mixer(u_gdn, u_sm, u_smt, 0, nb_p, l_p, gdn_p, zeros_state)
        y_gdn_s, _ = gdn_mixer(u_gdn, u_sm, u_smt, t_p, nb_s, l_s, gdn_p, state_gdn[:, i])
        y_fn_p = fnet_mixer(u_fn, 0, nb_p, l_p, dft_p)
        y_fn_s = fnet_mixer(u_fn, t_p, nb_s, l_s, dft_s)
        y_att_p, k_new, v_new = ctx_attention(u_att, 0, nb_p, l_p, att_p)
        y_att_s = latent_attention(u_att, t_p, nb_s, l_s, att_p,
                                   cache_k[:, i].reshape(nb_s, past, 2 * HEAD_DIM),
                                   cache_v[:, i].reshape(nb_s, past, 2 * HEAD_DIM), rope)
        ys = [jnp.concatenate([a, b], axis=0) for a, b in
              ((y_ssd_p, y_ssd_s), (y_gdn_p, y_gdn_s), (y_fn_p, y_fn_s), (y_att_p, y_att_s))]

        x1, h2, h2t = out_proj(ys, x, w_out[i].astype(bf16), g1, norm2_w[i][None, :], sc2, sh2, t_p, l_s)
        route = peer_route(h2, peer_w_q[i].astype(bf16), peer_keys[i])
        x = peer_experts(h2t, peer_u[i].astype(bf16), peer_v[i].astype(bf16), route, x1, g2, t_p, l_s)

        ks.append(k_new.reshape(nb_p, l_p, 2, HEAD_DIM))
        vs.append(v_new.reshape(nb_p, l_p, 2, HEAD_DIM))
        ssd_states.append(st_ssd)
        gdn_states.append(st_gdn)

    return (x[:t_p].reshape(nb_p, l_p, d), x[t_p:].reshape(nb_s, l_s, d),
            jnp.stack(ks, axis=1), jnp.stack(vs, axis=1),
            jnp.stack(ssd_states, axis=1), jnp.stack(gdn_states, axis=1))
```

```python
import functools
import math

import jax
import jax.numpy as jnp
from jax import lax
from jax.experimental import pallas as pl
from jax.experimental.pallas import tpu as pltpu

f32 = jnp.float32
bf16 = jnp.bfloat16
HI = lax.Precision.HIGHEST

D_MODEL = 1024
DEPTH = 4
EPS = 1e-6
NEG_INF = -1e30
CONV_W = 5
HEAD_DIM = 64
N_HEADS = 4
MIX = 256
SSD_CHUNK = 256
GDN_CHUNK = 64
WINDOW = 128
ABLOCK = 128
GRID_W = 64
ROPE_THETA = 10000.0
PEER_HEADS = 8
PEER_NKEYS = 128
PEER_TOPK = 16
PEER_EXPERTS = PEER_NKEYS * PEER_NKEYS

VMEM_LIMIT = 56 * 1024 * 1024

U_SSD, U_GDN, U_FN, U_ATT, U_SM = 768, 1024, 256, 512, 128
U_MAIN = U_SSD + U_GDN + U_FN + U_ATT


def _dot(a, b, prec=None):
    return jnp.dot(a, b, preferred_element_type=f32, precision=prec)


def _dot_nt(a, b, prec=None):
    return lax.dot_general(a, b, (((1,), (1,)), ((), ())), preferred_element_type=f32, precision=prec)


def _dot_tn(a, b, prec=None):
    return lax.dot_general(a, b, (((0,), (0,)), ((), ())), preferred_element_type=f32, precision=prec)


def _split_bf16(x):
    hi = x.astype(bf16)
    return hi, (x - hi.astype(f32)).astype(bf16)


def _dot3(a, b):
    ah, al = _split_bf16(a)
    bh, bl = _split_bf16(b)
    return _dot(ah, bh) + (_dot(ah, bl) + _dot(al, bh))


def _sigmoid(x):
    return 1.0 / (1.0 + jnp.exp(-x))


def _silu(x):
    return x * _sigmoid(x)


def _softplus(x):
    return jnp.maximum(x, 0.0) + jnp.log(1.0 + jnp.exp(-jnp.abs(x)))


def _params(*sem):
    return pltpu.CompilerParams(dimension_semantics=sem, vmem_limit_bytes=VMEM_LIMIT)


def _idiv(x, n):
    return lax.shift_right_logical(x, int(math.log2(n)))


def _head_mean_matrix(width):
    r = _idiv(lax.broadcasted_iota(jnp.int32, (width, width), 0), HEAD_DIM)
    c = _idiv(lax.broadcasted_iota(jnp.int32, (width, width), 1), HEAD_DIM)
    return jnp.where(r == c, 1.0 / HEAD_DIM, 0.0).astype(f32)


def _shift_rows(x, d, n):
    row = lax.broadcasted_iota(jnp.int32, x.shape, 0)
    y = pltpu.roll(x, (-d) % n, 0)
    ok = (row + d >= 0) & (row + d < n)
    return jnp.where(ok, y, 0.0)


def _conv_silu(x, w_ref, b_ref, n):
    half = (CONV_W - 1) // 2
    acc = b_ref[...] + w_ref[half:half + 1, :] * x
    for k in range(CONV_W):
        if k != half:
            acc = acc + w_ref[k:k + 1, :] * _shift_rows(x, k - half, n)
    return _silu(acc)


def _mod_body(c_ref, w_ref, b_ref, o_ref):
    o_ref[0] = _dot(_silu(c_ref[...]), w_ref[0], HI) + b_ref[0]


def adaln_all(cvecs, w_mod, b_mod):
    tn = 1536
    n = w_mod.shape[-1]
    return pl.pallas_call(
        _mod_body,
        grid=(DEPTH, n // tn),
        in_specs=[pl.BlockSpec((8, D_MODEL), lambda l, j: (0, 0)),
                  pl.BlockSpec((1, D_MODEL, tn), lambda l, j: (l, 0, j)),
                  pl.BlockSpec((1, 1, tn), lambda l, j: (l, 0, j))],
        out_specs=pl.BlockSpec((1, 8, tn), lambda l, j: (l, 0, j)),
        out_shape=jax.ShapeDtypeStruct((DEPTH, 8, n), f32),
        compiler_params=_params("arbitrary", "arbitrary"),
        name="adaln",
    )(cvecs, w_mod, b_mod.reshape(DEPTH, 1, n))


def _in_body(x_ref, nw_ref, sc_ref, sh_ref, w_ref, ws_ref, wst_ref,
             o_ssd, o_gdn, o_fn, o_att, o_sm, o_smt):
    x = x_ref[...]
    h = x * lax.rsqrt(jnp.mean(x * x, axis=-1, keepdims=True) + EPS) * nw_ref[...]
    h = h * (1.0 + sc_ref[0]) + sh_ref[0]
    u = _dot(h.astype(bf16), w_ref[...])
    o_ssd[...] = u[:, 0:U_SSD]
    o_gdn[...] = u[:, U_SSD:U_SSD + U_GDN]
    o_fn[...] = u[:, U_SSD + U_GDN:U_SSD + U_GDN + U_FN]
    o_att[...] = u[:, U_SSD + U_GDN + U_FN:U_MAIN]
    o_sm[...] = _dot(h, ws_ref[...], HI)
    o_smt[...] = _dot_nt(wst_ref[...], h, HI)


def _mod_row_map(tm, t_p, l_s):
    def f(i):
        t0 = i * tm
        return (jnp.where(t0 < t_p, 0, 1 + (t0 - t_p) // l_s), 0, 0)
    return f


def in_proj(x, nw, sc, sh, w_main, w_sm, w_smt, t_p, l_s):
    t = x.shape[0]
    tm = 512
    row = _mod_row_map(tm, t_p, l_s)
    widths = (U_SSD, U_GDN, U_FN, U_ATT, U_SM)
    return pl.pallas_call(
        _in_body,
        grid=(t // tm,),
        in_specs=[pl.BlockSpec((tm, D_MODEL), lambda i: (i, 0)),
                  pl.BlockSpec((1, D_MODEL), lambda i: (0, 0)),
                  pl.BlockSpec((1, 1, D_MODEL), row),
                  pl.BlockSpec((1, 1, D_MODEL), row),
                  pl.BlockSpec((D_MODEL, U_MAIN), lambda i: (0, 0)),
                  pl.BlockSpec((D_MODEL, U_SM), lambda i: (0, 0)),
                  pl.BlockSpec((U_SM, D_MODEL), lambda i: (0, 0))],
        out_specs=[pl.BlockSpec((tm, w), lambda i: (i, 0)) for w in widths]
        + [pl.BlockSpec((U_SM, tm), lambda i: (0, i))],
        out_shape=[jax.ShapeDtypeStruct((t, w), f32) for w in widths]
        + [jax.ShapeDtypeStruct((U_SM, t), f32)],
        compiler_params=_params("arbitrary"),
        name="in_proj",
    )(x, nw, sc, sh, w_main, w_sm, w_smt)


def _ssd_body(L, u_ref, sm_ref, smt_ref, cw_ref, cb_ref, alr_ref, alc_ref, dbr_ref, dbc_ref,
              dsk_ref, nw_ref, h0_ref, y_ref, hT_ref, yacc):
    Q = min(L, SSD_CHUNK)
    nc = L // Q
    hd = HEAD_DIM
    z = u_ref[:, 0:MIX]
    xbc = _conv_silu(u_ref[:, MIX:U_SSD], cw_ref, cb_ref, L)
    xs = xbc[:, 0:MIX]
    dtc = _softplus(sm_ref[:, 0:8] + dbr_ref[...])
    dtr = _softplus(smt_ref[0:8, :] + dbc_ref[...])
    ac = dtc * (-jnp.exp(alr_ref[...]))
    ar = dtr * (-jnp.exp(alc_ref[...]))
    ri = lax.broadcasted_iota(jnp.int32, (Q, Q), 0)
    ci = lax.broadcasted_iota(jnp.int32, (Q, Q), 1)
    low = ri >= ci
    upp = ci >= ri
    tril = jnp.where(low, 1.0, 0.0).astype(f32)

    def chunk_terms(c):
        r0 = c * Q
        a_c = ac[r0:r0 + Q, :]
        a_r = ar[:, r0:r0 + Q]
        pc = _dot(tril, a_c, HI)
        pr = _dot_nt(a_r, tril, HI)
        return a_c, a_r, pc, pr

    hf = [h0_ref[0, 0, h] for h in range(N_HEADS)]
    for c in range(nc):
        r0 = c * Q
        a_c, a_r, pc, pr = chunk_terms(c)
        ys = []
        for h in range(N_HEADS):
            g = h // 2
            x_h = xs[r0:r0 + Q, h * hd:(h + 1) * hd]
            b_g = xbc[r0:r0 + Q, MIX + g * hd:MIX + (g + 1) * hd]
            c_g = xbc[r0:r0 + Q, MIX + 2 * hd + g * hd:MIX + 2 * hd + (g + 1) * hd]
            gm = _dot_nt(c_g, b_g, HI)
            hb = N_HEADS + h
            seg_f = pc[:, h:h + 1] - pr[h:h + 1, :]
            l_f = jnp.where(low, jnp.exp(jnp.minimum(seg_f, 0.0)), 0.0)
            e_c = pc[:, hb:hb + 1] - a_c[:, hb:hb + 1]
            e_r = pr[hb:hb + 1, :] - a_r[hb:hb + 1, :]
            seg_b = e_r - e_c
            l_b = jnp.where(upp, jnp.exp(jnp.minimum(seg_b, 0.0)), 0.0)
            m = gm * (l_f * dtr[h:h + 1, r0:r0 + Q] + l_b * dtr[hb:hb + 1, r0:r0 + Q])
            y_h = _dot(m, x_h, HI)
            y_h = y_h + _dot_nt(c_g * jnp.exp(pc[:, h:h + 1]), hf[h], HI)
            tot = pc[Q - 1:Q, h:h + 1]
            wgt = dtc[r0:r0 + Q, h:h + 1] * jnp.exp(tot - pc[:, h:h + 1])
            hf[h] = hf[h] * jnp.exp(tot) + _dot_tn(x_h * wgt, b_g, HI)
            ys.append(y_h)
        yacc[r0:r0 + Q, :] = jnp.concatenate(ys, axis=1)
    hb_s = [h0_ref[0, 1, h] for h in range(N_HEADS)]
    for c in range(nc - 1, -1, -1):
        r0 = c * Q
        a_c, a_r, pc, pr = chunk_terms(c)
        ys = []
        for h in range(N_HEADS):
            g = h // 2
            hb = N_HEADS + h
            x_h = xs[r0:r0 + Q, h * hd:(h + 1) * hd]
            b_g = xbc[r0:r0 + Q, MIX + g * hd:MIX + (g + 1) * hd]
            c_g = xbc[r0:r0 + Q, MIX + 2 * hd + g * hd:MIX + 2 * hd + (g + 1) * hd]
            e_c = pc[:, hb:hb + 1] - a_c[:, hb:hb + 1]
            tot = pc[Q - 1:Q, hb:hb + 1]
            ys.append(_dot_nt(c_g * jnp.exp(tot - e_c), hb_s[h], HI))
            wgt = dtc[r0:r0 + Q, hb:hb + 1] * jnp.exp(e_c)
            hb_s[h] = hb_s[h] * jnp.exp(tot) + _dot_tn(x_h * wgt, b_g, HI)
        yacc[r0:r0 + Q, :] = yacc[r0:r0 + Q, :] + jnp.concatenate(ys, axis=1)
    for h in range(N_HEADS):
        hT_ref[0, 0, h] = hf[h]
        hT_ref[0, 1, h] = hb_s[h]
    y = yacc[...] + dsk_ref[...] * xs
    y = y * _silu(z)
    ms = _dot(y * y, _head_mean_matrix(MIX), HI)
    y_ref[...] = (y * lax.rsqrt(ms + EPS) * nw_ref[...]).astype(y_ref.dtype)


def ssd_mixer(u_ssd, u_sm, u_smt, t0, nseq, L, p, h0):
    b0 = t0 // L
    small = lambda s: pl.BlockSpec(s, lambda b: (0,) * len(s))
    return pl.pallas_call(
        functools.partial(_ssd_body, L),
        grid=(nseq,),
        in_specs=[pl.BlockSpec((L, U_SSD), lambda b: (b0 + b, 0)),
                  pl.BlockSpec((L, U_SM), lambda b: (b0 + b, 0)),
                  pl.BlockSpec((U_SM, L), lambda b: (0, b0 + b)),
                  small((CONV_W, 512)), small((1, 512)),
                  small((1, 8)), small((8, 1)), small((1, 8)), small((8, 1)),
                  small((1, MIX)), small((1, MIX)),
                  pl.BlockSpec((1, 2, N_HEADS, HEAD_DIM, HEAD_DIM), lambda b: (b, 0, 0, 0, 0))],
        out_specs=[pl.BlockSpec((L, MIX), lambda b: (b, 0)),
                   pl.BlockSpec((1, 2, N_HEADS, HEAD_DIM, HEAD_DIM), lambda b: (b, 0, 0, 0, 0))],
        out_shape=[jax.ShapeDtypeStruct((nseq * L, MIX), bf16),
                   jax.ShapeDtypeStruct((nseq, 2, N_HEADS, HEAD_DIM, HEAD_DIM), f32)],
        scratch_shapes=[pltpu.VMEM((L, MIX), f32)],
        compiler_params=_params("arbitrary"),
        name=f"ssd_{L}",
    )(u_ssd, u_sm, u_smt, p["conv_w"], p["conv_b"], p["al_row"], p["al_col"], p["db_row"], p["db_col"],
      p["d_skip"], p["norm_w"], h0)


def _gdn_body(L, u_ref, sm_ref, smt_ref, cw_ref, cb_ref, alr_ref, alc_ref, dbr_ref, dbc_ref,
              nw_ref, s0_ref, y_ref, sT_ref, qkv, oacc):
    Q = GDN_CHUNK
    nc = L // Q
    W = MIX
    qkv[...] = _conv_silu(u_ref[:, 0:3 * MIX], cw_ref, cb_ref, L)
    hm = _head_mean_matrix(MIX) * float(HEAD_DIM)
    q_all = qkv[:, 0:MIX]
    k_all = qkv[:, MIX:2 * MIX]
    qkv[:, 0:MIX] = q_all * lax.rsqrt(_dot(q_all * q_all, hm, HI) + EPS) * (HEAD_DIM ** -0.5)
    qkv[:, MIX:2 * MIX] = k_all * lax.rsqrt(_dot(k_all * k_all, hm, HI) + EPS)
    lac = -jnp.exp(alr_ref[...]) * _softplus(sm_ref[:, 8:16] + dbr_ref[...])
    lar = -jnp.exp(alc_ref[...]) * _softplus(smt_ref[8:16, :] + dbc_ref[...])
    btc = _sigmoid(sm_ref[:, 16:24])
    btr = _sigmoid(smt_ref[16:24, :])

    ri = lax.broadcasted_iota(jnp.int32, (W, W), 0)
    ci = lax.broadcasted_iota(jnp.int32, (W, W), 1)
    same_head = _idiv(ri, Q) == _idiv(ci, Q)
    blk16 = _idiv(ri, 16) == _idiv(ci, 16)
    eye = jnp.where(ri == ci, 1.0, 0.0).astype(f32)
    rq = lax.broadcasted_iota(jnp.int32, (Q, Q), 0)
    cq = lax.broadcasted_iota(jnp.int32, (Q, Q), 1)
    tril_q = jnp.where(rq >= cq, 1.0, 0.0).astype(f32)
    triu_q = jnp.where(cq >= rq, 1.0, 0.0).astype(f32)

    def stack_cols(m, base):
        return jnp.concatenate([m[:, base + h:base + h + 1] for h in range(N_HEADS)], axis=0)

    def stack_rows(m, base):
        return jnp.concatenate([m[base + h:base + h + 1, :] for h in range(N_HEADS)], axis=1)

    def block_diag(m):
        return jnp.where(same_head, jnp.concatenate([m] * N_HEADS, axis=1), 0.0)

    def heads_to_rows(m):
        return jnp.concatenate([m[:, h * HEAD_DIM:(h + 1) * HEAD_DIM] for h in range(N_HEADS)], axis=0)

    def rows_to_heads(m):
        return jnp.concatenate([m[h * Q:(h + 1) * Q, :] for h in range(N_HEADS)], axis=1)

    def sweep(d, S):
        tri_mat = tril_q if d == 0 else triu_q
        causal = (ri >= ci) if d == 0 else (ci >= ri)
        strict = (ri > ci) if d == 0 else (ci > ri)
        order = range(nc) if d == 0 else range(nc - 1, -1, -1)
        for c in order:
            r0 = c * Q
            qs = heads_to_rows(qkv[r0:r0 + Q, 0:MIX])
            ks = heads_to_rows(qkv[r0:r0 + Q, MIX:2 * MIX])
            vs = heads_to_rows(qkv[r0:r0 + Q, 2 * MIX:3 * MIX])
            la_c = lac[r0:r0 + Q, :]
            la_r = lar[:, r0:r0 + Q]
            gc = stack_cols(_dot(tri_mat, la_c, HI), 4 * d)
            gr = stack_rows(_dot_nt(la_r, tri_mat, HI), 4 * d)
            bc = stack_cols(btc[r0:r0 + Q, :], 4 * d)
            ends = []
            for h in range(N_HEADS):
                e = gr[:, h * Q + Q - 1:h * Q + Q] if d == 0 else gr[:, h * Q:h * Q + 1]
                ends.append(jnp.broadcast_to(e, (Q, 1)))
            g_end = jnp.concatenate(ends, axis=0)
            decay = jnp.where(causal & same_head, jnp.exp(jnp.minimum(gc - gr, 0.0)), 0.0)
            kbd = block_diag(ks).astype(bf16)
            kk = _dot_nt(kbd, kbd)
            a = jnp.where(strict, bc * kk * decay, 0.0)
            n = jnp.where(blk16, a, 0.0)
            n2 = _dot3(n, n)
            n4 = _dot3(n2, n2)
            n8 = _dot3(n4, n4)
            dinv = _dot3(_dot3(eye - n, eye + n2), _dot3(eye + n4, eye + n8))
            zz = _dot3(dinv, a - n)
            z2 = _dot3(zz, zz)
            tinv = _dot3(_dot3(eye - zz, eye + z2), dinv)
            rhs = jnp.concatenate([vs * bc, ks * (bc * jnp.exp(gc))], axis=1)
            sol = _dot3(tinv, rhs)
            u_c = sol[:, 0:HEAD_DIM]
            w_c = sol[:, HEAD_DIM:2 * HEAD_DIM]
            s_b = S.astype(bf16)
            attn = jnp.where(causal, _dot_nt(block_diag(qs).astype(bf16), kbd) * decay, 0.0)
            v_new = u_c - _dot(block_diag(w_c).astype(bf16), s_b)
            v_b = v_new.astype(bf16)
            o = _dot(block_diag(qs * jnp.exp(gc)).astype(bf16), s_b) + _dot(attn.astype(bf16), v_b)
            S = S * jnp.exp(g_end) + _dot_tn(block_diag(ks * jnp.exp(g_end - gc)).astype(bf16), v_b)
            o = rows_to_heads(o)
            if d == 0:
                oacc[r0:r0 + Q, :] = o
            else:
                oacc[r0:r0 + Q, :] = oacc[r0:r0 + Q, :] + o
        return S

    for d in range(2):
        S0 = jnp.concatenate([s0_ref[0, d, h] for h in range(N_HEADS)], axis=0)
        S = sweep(d, S0)
        for h in range(N_HEADS):
            sT_ref[0, d, h] = S[h * HEAD_DIM:(h + 1) * HEAD_DIM, :]
    o = oacc[...]
    ms = _dot(o * o, _head_mean_matrix(MIX), HI)
    o = o * lax.rsqrt(ms + EPS) * nw_ref[...]
    y_ref[...] = (o * _silu(u_ref[:, 3 * MIX:4 * MIX])).astype(y_ref.dtype)


def gdn_mixer(u_gdn, u_sm, u_smt, t0, nseq, L, p, s0):
    b0 = t0 // L
    small = lambda s: pl.BlockSpec(s, lambda b: (0,) * len(s))
    return pl.pallas_call(
        functools.partial(_gdn_body, L),
        grid=(nseq,),
        in_specs=[pl.BlockSpec((L, U_GDN), lambda b: (b0 + b, 0)),
                  pl.BlockSpec((L, U_SM), lambda b: (b0 + b, 0)),
                  pl.BlockSpec((U_SM, L), lambda b: (0, b0 + b)),
                  small((CONV_W, 3 * MIX)), small((1, 3 * MIX)),
                  small((1, 8)), small((8, 1)), small((1, 8)), small((8, 1)),
                  small((1, MIX)),
                  pl.BlockSpec((1, 2, N_HEADS, HEAD_DIM, HEAD_DIM), lambda b: (b, 0, 0, 0, 0))],
        out_specs=[pl.BlockSpec((L, MIX), lambda b: (b, 0)),
                   pl.BlockSpec((1, 2, N_HEADS, HEAD_DIM, HEAD_DIM), lambda b: (b, 0, 0, 0, 0))],
        out_shape=[jax.ShapeDtypeStruct((nseq * L, MIX), bf16),
                   jax.ShapeDtypeStruct((nseq, 2, N_HEADS, HEAD_DIM, HEAD_DIM), f32)],
        scratch_shapes=[pltpu.VMEM((L, 3 * MIX), f32), pltpu.VMEM((L, MIX), f32)],
        compiler_params=_params("arbitrary"),
        name=f"gdn_{L}",
    )(u_gdn, u_sm, u_smt, p["conv_w"], p["conv_b"], p["al_row"], p["al_col"], p["db_row"], p["db_col"],
      p["norm_w"], s0)


def _fnet_body(x_ref, cl_ref, sl_ref, cc_ref, sc_ref, y_ref):
    x = x_ref[...]
    xc = _dot(x, cc_ref[...], HI)
    xs = _dot(x, sc_ref[...], HI)
    y_ref[...] = (_dot(cl_ref[...], xc, HI) - _dot(sl_ref[...], xs, HI)).astype(y_ref.dtype)


def fnet_mixer(u_fn, t0, nseq, L, tabs):
    b0 = t0 // L
    full = lambda s: pl.BlockSpec(s, lambda b: (0,) * len(s))
    return pl.pallas_call(
        _fnet_body,
        grid=(nseq,),
        in_specs=[pl.BlockSpec((L, MIX), lambda b: (b0 + b, 0)),
                  full((L, L)), full((L, L)), full((MIX, MIX)), full((MIX, MIX))],
        out_specs=pl.BlockSpec((L, MIX), lambda b: (b, 0)),
        out_shape=jax.ShapeDtypeStruct((nseq * L, MIX), bf16),
        compiler_params=_params("arbitrary"),
        name=f"fnet_{L}",
    )(u_fn, *tabs)


def dft_tables(L):
    n = jnp.arange(L, dtype=jnp.int32)
    ang_l = (2.0 * math.pi / L) * ((n[:, None] * n[None, :]) % L).astype(f32)
    m = jnp.arange(MIX, dtype=jnp.int32)
    same = (m[:, None] // HEAD_DIM) == (m[None, :] // HEAD_DIM)
    ang_c = (2.0 * math.pi / HEAD_DIM) * (((m[:, None] % HEAD_DIM) * (m[None, :] % HEAD_DIM)) % HEAD_DIM).astype(f32)
    sl = 1.0 / math.sqrt(L)
    sc = 1.0 / math.sqrt(HEAD_DIM)
    return (jnp.cos(ang_l) * sl, jnp.sin(ang_l) * sl,
            jnp.where(same, jnp.cos(ang_c) * sc, 0.0), jnp.where(same, jnp.sin(ang_c) * sc, 0.0))


def _qk_norm(x, w_row, width):
    ms = _dot(x * x, _head_mean_matrix(width), HI)
    return x * lax.rsqrt(ms + EPS) * w_row


def _sink_softmax_pv(s_list, v_list, sink):
    m = sink
    for s in s_list:
        m = jnp.maximum(m, jnp.max(s, axis=-1, keepdims=True))
    den = jnp.exp(sink - m)
    acc = None
    for s, v in zip(s_list, v_list):
        e = jnp.exp(s - m)
        den = den + jnp.sum(e, axis=-1, keepdims=True)
        pv = _dot(e, v, HI)
        acc = pv if acc is None else acc + pv
    return acc / den


def _ctx_att_body(L, u_ref, qw_ref, kw_ref, sink_ref, y_ref, k_ref, v_ref):
    hd = HEAD_DIM
    scale = hd ** -0.5
    q = _qk_norm(u_ref[:, 0:MIX], qw_ref[...], MIX)
    k = _qk_norm(u_ref[:, MIX:MIX + 2 * hd], kw_ref[...], 2 * hd)
    v = u_ref[:, MIX + 2 * hd:MIX + 4 * hd]
    k_ref[...] = k
    v_ref[...] = v
    outs = []
    for g in range(2):
        k_g = k[:, g * hd:(g + 1) * hd]
        v_g = v[:, g * hd:(g + 1) * hd]
        for r in range(2):
            h = 2 * g + r
            s = _dot_nt(q[:, h * hd:(h + 1) * hd], k_g, HI) * scale
            outs.append(_sink_softmax_pv([s], [v_g], sink_ref[0:1, h:h + 1]))
    y_ref[...] = jnp.concatenate(outs, axis=1).astype(y_ref.dtype)


def ctx_attention(u_att, t0, nseq, L, p):
    b0 = t0 // L
    small = lambda s: pl.BlockSpec(s, lambda b: (0,) * len(s))
    return pl.pallas_call(
        functools.partial(_ctx_att_body, L),
        grid=(nseq,),
        in_specs=[pl.BlockSpec((L, U_ATT), lambda b: (b0 + b, 0)),
                  small((1, MIX)), small((1, 2 * HEAD_DIM)), small((1, N_HEADS))],
        out_specs=[pl.BlockSpec((L, MIX), lambda b: (b, 0)),
                   pl.BlockSpec((L, 2 * HEAD_DIM), lambda b: (b, 0)),
                   pl.BlockSpec((L, 2 * HEAD_DIM), lambda b: (b, 0))],
        out_shape=[jax.ShapeDtypeStruct((nseq * L, MIX), bf16),
                   jax.ShapeDtypeStruct((nseq * L, 2 * HEAD_DIM), f32),
                   jax.ShapeDtypeStruct((nseq * L, 2 * HEAD_DIM), f32)],
        compiler_params=_params("arbitrary"),
        name="ctx_att",
    )(u_att, p["q_norm"], p["k_norm"], p["sink"])


def _rope(x, cos_ref, sin_ref, width):
    lane = lax.broadcasted_iota(jnp.int32, x.shape, 1) & (HEAD_DIM - 1)
    half = HEAD_DIM // 2
    swapped = jnp.where(lane < half, pltpu.roll(x, width - half, 1), pltpu.roll(x, half, 1))
    return x * cos_ref[...] + swapped * sin_ref[...]


def _lat_att_body(L, P, u_ref, kc_ref, vc_ref, qw_ref, kw_ref, sink_ref, cq_ref, sq_ref, ck_ref, sk_ref, y_ref):
    hd = HEAD_DIM
    scale = hd ** -0.5
    nb = L // ABLOCK
    q = _rope(_qk_norm(u_ref[:, 0:MIX], qw_ref[...], MIX), cq_ref, sq_ref, MIX)
    k = _rope(_qk_norm(u_ref[:, MIX:MIX + 2 * hd], kw_ref[...], 2 * hd), ck_ref, sk_ref, 2 * hd)
    v = u_ref[:, MIX + 2 * hd:MIX + 4 * hd]
    kc = kc_ref[0]
    vc = vc_ref[0]
    for i in range(nb):
        lo = max(i - 1, 0) * ABLOCK
        hi = min(i + 2, nb) * ABLOCK
        qpos = i * ABLOCK + lax.broadcasted_iota(jnp.int32, (ABLOCK, hi - lo), 0)
        kpos = lo + lax.broadcasted_iota(jnp.int32, (ABLOCK, hi - lo), 1)
        dist = qpos - kpos
        ok = (dist <= WINDOW) & (dist >= -WINDOW)
        outs = []
        for g in range(2):
            k_l = k[lo:hi, g * hd:(g + 1) * hd]
            v_l = v[lo:hi, g * hd:(g + 1) * hd]
            k_c = kc[:, g * hd:(g + 1) * hd]
            v_c = vc[:, g * hd:(g + 1) * hd]
            for r in range(2):
                h = 2 * g + r
                q_h = q[i * ABLOCK:(i + 1) * ABLOCK, h * hd:(h + 1) * hd]
                s_loc = jnp.where(ok, _dot_nt(q_h, k_l, HI) * scale, NEG_INF)
                s_ctx = _dot_nt(q_h, k_c, HI) * scale
                outs.append(_sink_softmax_pv([s_loc, s_ctx], [v_l, v_c], sink_ref[0:1, h:h + 1]))
        y_ref[i * ABLOCK:(i + 1) * ABLOCK, :] = jnp.concatenate(outs, axis=1).astype(y_ref.dtype)


def latent_attention(u_att, t0, nseq, L, p, kc, vc, rope):
    b0 = t0 // L
    P = kc.shape[1]
    small = lambda s: pl.BlockSpec(s, lambda b: (0,) * len(s))
    return pl.pallas_call(
        functools.partial(_lat_att_body, L, P),
        grid=(nseq,),
        in_specs=[pl.BlockSpec((L, U_ATT), lambda b: (b0 + b, 0)),
                  pl.BlockSpec((1, P, 2 * HEAD_DIM), lambda b: (b, 0, 0)),
                  pl.BlockSpec((1, P, 2 * HEAD_DIM), lambda b: (b, 0, 0)),
                  small((1, MIX)), small((1, 2 * HEAD_DIM)), small((1, N_HEADS)),
                  small((L, MIX)), small((L, MIX)), small((L, 2 * HEAD_DIM)), small((L, 2 * HEAD_DIM))],
        out_specs=pl.BlockSpec((L, MIX), lambda b: (b, 0)),
        out_shape=jax.ShapeDtypeStruct((nseq * L, MIX), bf16),
        compiler_params=_params("arbitrary"),
        name="lat_att",
    )(u_att, kc, vc, p["q_norm"], p["k_norm"], p["sink"], *rope)


def rope_tables(L):
    rows = L // GRID_W
    row = jnp.repeat(jnp.arange(rows, dtype=f32), GRID_W)
    col = jnp.tile(jnp.arange(GRID_W, dtype=f32), rows)
    nf = HEAD_DIM // 4
    inv = ROPE_THETA ** (-jnp.arange(nf, dtype=f32) / nf)
    ang = jnp.concatenate([row[:, None] * inv, col[:, None] * inv], axis=-1)
    cos = jnp.concatenate([jnp.cos(ang), jnp.cos(ang)], axis=-1)
    sin = jnp.concatenate([-jnp.sin(ang), jnp.sin(ang)], axis=-1)
    return (jnp.tile(cos, (1, N_HEADS)), jnp.tile(sin, (1, N_HEADS)), jnp.tile(cos, (1, 2)), jnp.tile(sin, (1, 2)))


def _out_body(ya, yb, yc, yd, x_ref, w_ref, g1_ref, nw_ref, sc_ref, sh_ref, x1_ref, h2_ref, h2t_ref):
    y = _dot(ya[...], w_ref[0:MIX, :])
    y = y + _dot(yb[...], w_ref[MIX:2 * MIX, :])
    y = y + _dot(yc[...], w_ref[2 * MIX:3 * MIX, :])
    y = y + _dot(yd[...], w_ref[3 * MIX:4 * MIX, :])
    x1 = x_ref[...] + g1_ref[0] * y
    x1_ref[...] = x1
    h = x1 * lax.rsqrt(jnp.mean(x1 * x1, axis=-1, keepdims=True) + EPS) * nw_ref[...]
    h = h * (1.0 + sc_ref[0]) + sh_ref[0]
    h2_ref[...] = h.astype(bf16)
    h2t_ref[...] = h.T.astype(bf16)


def out_proj(ys, x, w_out, g1, nw, sc, sh, t_p, l_s):
    t = x.shape[0]
    tm = 512
    row = _mod_row_map(tm, t_p, l_s)
    tok = lambda w: pl.BlockSpec((tm, w), lambda i: (i, 0))
    return pl.pallas_call(
        _out_body,
        grid=(t // tm,),
        in_specs=[tok(MIX), tok(MIX), tok(MIX), tok(MIX), tok(D_MODEL),
                  pl.BlockSpec((D_MODEL, D_MODEL), lambda i: (0, 0)),
                  pl.BlockSpec((1, 1, D_MODEL), row),
                  pl.BlockSpec((1, D_MODEL), lambda i: (0, 0)),
                  pl.BlockSpec((1, 1, D_MODEL), row),
                  pl.BlockSpec((1, 1, D_MODEL), row)],
        out_specs=[tok(D_MODEL), tok(D_MODEL), pl.BlockSpec((D_MODEL, tm), lambda i: (0, i))],
        out_shape=[jax.ShapeDtypeStruct((t, D_MODEL), f32),
                   jax.ShapeDtypeStruct((t, D_MODEL), bf16),
                   jax.ShapeDtypeStruct((D_MODEL, t), bf16)],
        compiler_params=_params("arbitrary"),
        name="out_proj",
    )(*ys, x, w_out, g1, nw, sc, sh)


def _top16(s, n):
    io = lax.broadcasted_iota(jnp.int32, s.shape, 0)
    rank = jnp.full(s.shape, PEER_TOPK, jnp.int32)
    vals = []
    for r in range(PEER_TOPK):
        m = jnp.max(s, axis=0, keepdims=True)
        idx = jnp.min(jnp.where(s == m, io, n), axis=0, keepdims=True)
        hit = io == idx
        rank = jnp.where(hit, r, rank)
        s = jnp.where(hit, -jnp.inf, s)
        vals.append(m)
    return jnp.concatenate(vals, axis=0), rank


def _peer_route_body(h2_ref, wq_ref, keys_ref, r2_ref, e2_ref, lim_ref, cw_ref):
    k = PEER_TOPK
    q = _dot(h2_ref[...], wq_ref[...])
    tn = q.shape[0]
    n_b = [k // (a + 1) for a in range(k)]
    n_cand = sum(n_b)
    n_rows = -(-n_cand // 8) * 8
    ra = lax.broadcasted_iota(jnp.int32, (k, n_rows), 0)
    rc = lax.broadcasted_iota(jnp.int32, (k, n_rows), 1)
    rep_a = jnp.zeros((k, n_rows), f32)
    start = 0
    for a in range(k):
        rep_a = jnp.where((ra == a) & (rc >= start) & (rc < start + n_b[a]), 1.0, rep_a)
        start += n_b[a]
    pad = jnp.full((n_rows - n_cand, tn), -jnp.inf, f32)
    for h in range(PEER_HEADS):
        s1 = _dot_nt(keys_ref[h, 0], q[:, h * 256:h * 256 + 128], HI)
        s2 = _dot_nt(keys_ref[h, 1], q[:, h * 256 + 128:h * 256 + 256], HI)
        t1, rank1 = _top16(s1, PEER_NKEYS)
        t2, rank2 = _top16(s2, PEER_NKEYS)
        cand = jnp.concatenate([t1[a:a + 1, :] + t2[0:n_b[a], :] for a in range(k)] + [pad], axis=0)
        best, crank = _top16(cand, n_rows)
        sel = jnp.where(crank < k, 1.0, 0.0).astype(f32)
        count_a = _dot(rep_a, sel)
        zsum = jnp.sum(jnp.exp(best - best[0:1, :]), axis=0, keepdims=True)
        lim = jnp.zeros((PEER_NKEYS, tn), f32)
        for a in range(k):
            lim = jnp.where(rank1 == a, count_a[a:a + 1, :], lim)
        r2_ref[h] = rank2.astype(f32).astype(bf16)
        e2_ref[h] = jnp.exp(s2 - t2[0:1, :]).astype(bf16)
        lim_ref[h] = lim
        cw_ref[h] = jnp.exp(s1 - t1[0:1, :]) / zsum


def peer_route(h2, w_q, keys):
    t = h2.shape[0]
    tn = 256
    arr = jax.ShapeDtypeStruct((PEER_HEADS, PEER_NKEYS, t), f32)
    arr_h = jax.ShapeDtypeStruct((PEER_HEADS, PEER_NKEYS, t), bf16)
    spec = pl.BlockSpec((PEER_HEADS, PEER_NKEYS, tn), lambda i: (0, 0, i))
    return pl.pallas_call(
        _peer_route_body,
        grid=(t // tn,),
        in_specs=[pl.BlockSpec((tn, D_MODEL), lambda i: (i, 0)),
                  pl.BlockSpec((D_MODEL, 2 * D_MODEL), lambda i: (0, 0)),
                  pl.BlockSpec((PEER_HEADS, 2, PEER_NKEYS, PEER_NKEYS), lambda i: (0, 0, 0, 0))],
        out_specs=[spec, spec, spec, spec],
        out_shape=[arr_h, arr_h, arr, arr],
        compiler_params=_params("arbitrary"),
        name="peer_route",
    )(h2, w_q, keys)


def _peer_expert_body(eb_i, h2t_ref, u_ref, v_ref, r2_ref, e2_ref, lim_ref, cw_ref, x1_ref, g2_ref,
                      o_ref, wt, acc):
    e = pl.program_id(1)

    @pl.when(e == 0)
    def _():
        acc[...] = jnp.zeros_like(acc)

    st = _dot(u_ref[...], h2t_ref[...])
    for ii in range(eb_i):
        i = e * eb_i + ii
        g = None
        for h in range(PEER_HEADS):
            lim = lim_ref[h, pl.ds(i, 1), :].astype(bf16)
            cw = cw_ref[h, pl.ds(i, 1), :].astype(bf16)
            term = jnp.where(r2_ref[h] < lim, e2_ref[h] * cw, jnp.zeros((), bf16))
            g = term if g is None else g + term
        s = st[ii * PEER_NKEYS:(ii + 1) * PEER_NKEYS, :]
        act = 0.5 * s * (1.0 + lax.erf(s * (2.0 ** -0.5)))
        wt[ii * PEER_NKEYS:(ii + 1) * PEER_NKEYS, :] = g * act.astype(bf16)
    acc[...] += _dot_tn(wt[...], v_ref[...])

    @pl.when(e == pl.num_programs(1) - 1)
    def _():
        o_ref[...] = x1_ref[...] + g2_ref[0] * acc[...]


def peer_experts(h2t, u_tab, v_tab, route, x1, g2, t_p, l_s):
    t = x1.shape[0]
    tn = 512
    eb_i = 4
    eb = eb_i * PEER_NKEYS
    row = _mod_row_map(tn, t_p, l_s)
    rspec = pl.BlockSpec((PEER_HEADS, PEER_NKEYS, tn), lambda j, e: (0, 0, j))
    return pl.pallas_call(
        functools.partial(_peer_expert_body, eb_i),
        grid=(t // tn, PEER_EXPERTS // eb),
        in_specs=[pl.BlockSpec((D_MODEL, tn), lambda j, e: (0, j)),
                  pl.BlockSpec((eb, D_MODEL), lambda j, e: (e, 0)),
                  pl.BlockSpec((eb, D_MODEL), lambda j, e: (e, 0)),
                  rspec, rspec, rspec, rspec,
                  pl.BlockSpec((tn, D_MODEL), lambda j, e: (j, 0)),
                  pl.BlockSpec((1, 1, D_MODEL), lambda j, e: row(j))],
        out_specs=pl.BlockSpec((tn, D_MODEL), lambda j, e: (j, 0)),
        out_shape=jax.ShapeDtypeStruct((t, D_MODEL), f32),
        scratch_shapes=[pltpu.VMEM((eb, tn), bf16), pltpu.VMEM((tn, D_MODEL), f32)],
        compiler_params=_params("arbitrary", "arbitrary"),
        name="peer_experts",
    )(h2t, u_tab, v_tab, *route, x1, g2)


def _regroup_w_in(w):
    o_gdn = 776
    o_fn = o_gdn + 1040
    o_att = o_fn + 256
    main = jnp.concatenate([w[:, 0:768], w[:, o_gdn:o_gdn + 1024], w[:, o_fn:o_fn + 256], w[:, o_att:o_att + 512]], axis=1)
    small = jnp.concatenate([w[:, 768:776], w[:, o_gdn + 1024:o_gdn + 1040],
                             jnp.zeros((w.shape[0], U_SM - 24), w.dtype)], axis=1)
    return main.astype(bf16), small, small.T


def _layer_params(i, ssd_conv_w, ssd_conv_b, ssd_A_log, ssd_dt_bias, ssd_D, ssd_norm_w,
                  gdn_conv_w, gdn_conv_b, gdn_A_log, gdn_dt_bias, gdn_norm_w, q_norm_w, k_norm_w, att_sink):
    ssd = dict(conv_w=ssd_conv_w[i], conv_b=ssd_conv_b[i][None, :],
               al_row=ssd_A_log[i].reshape(1, 8), al_col=ssd_A_log[i].reshape(8, 1),
               db_row=ssd_dt_bias[i].reshape(1, 8), db_col=ssd_dt_bias[i].reshape(8, 1),
               d_skip=jnp.repeat(ssd_D[i], HEAD_DIM)[None, :], norm_w=ssd_norm_w[i][None, :])
    gdn = dict(conv_w=gdn_conv_w[i], conv_b=gdn_conv_b[i][None, :],
               al_row=gdn_A_log[i].reshape(1, 8), al_col=gdn_A_log[i].reshape(8, 1),
               db_row=gdn_dt_bias[i].reshape(1, 8), db_col=gdn_dt_bias[i].reshape(8, 1),
               norm_w=jnp.tile(gdn_norm_w[i], N_HEADS)[None, :])
    att = dict(q_norm=jnp.tile(q_norm_w[i], N_HEADS)[None, :], k_norm=jnp.tile(k_norm_w[i], 2)[None, :],
               sink=att_sink[i][None, :])
    return ssd, gdn, att


def kernel(x_prompt, x_sample, cache_k, cache_v, state_ssd, state_gdn, c, c_ctx, w_mod, b_mod, norm1_w, norm2_w, w_in, w_out, ssd_conv_w, ssd_conv_b, ssd_A_log, ssd_dt_bias, ssd_D, ssd_norm_w, gdn_conv_w, gdn_conv_b, gdn_A_log, gdn_dt_bias, gdn_norm_w, q_norm_w, k_norm_w, att_sink, peer_w_q, peer_keys, peer_u, peer_v):
    nb_p, l_p, d = x_prompt.shape
    nb_s, l_s, _ = x_sample.shape
    past = cache_k.shape[2]
    t_p = nb_p * l_p
    x = jnp.concatenate([x_prompt.reshape(t_p, d), x_sample.reshape(nb_s * l_s, d)], axis=0)

    cvecs = jnp.concatenate([c_ctx[None, :], c, jnp.zeros((8 - 1 - nb_s, d), f32)], axis=0)
    mods = adaln_all(cvecs, w_mod, b_mod)
    dft_p, dft_s = dft_tables(l_p), dft_tables(l_s)
    rope = rope_tables(l_s)
    zeros_state = jnp.zeros((nb_p, 2, N_HEADS, HEAD_DIM, HEAD_DIM), f32)

    ks, vs, ssd_states, gdn_states = [], [], [], []
    for i in range(DEPTH):
        sh1, sc1, g1, sh2, sc2, g2 = [mods[i, :, j * d:(j + 1) * d].reshape(8, 1, d) for j in range(6)]
        ssd_p, gdn_p, att_p = _layer_params(i, ssd_conv_w, ssd_conv_b, ssd_A_log, ssd_dt_bias, ssd_D, ssd_norm_w,
                                            gdn_conv_w, gdn_conv_b, gdn_A_log, gdn_dt_bias, gdn_norm_w,
                                            q_norm_w, k_norm_w, att_sink)
        w_main, w_sm, w_smt = _regroup_w_in(w_in[i])
        u_ssd, u_gdn, u_fn, u_att, u_sm, u_smt = in_proj(x, norm1_w[i][None, :], sc1, sh1, w_main, w_sm, w_smt, t_p, l_s)

        y_ssd_p, st_ssd = ssd_mixer(u_ssd, u_sm, u_smt, 0, nb_p, l_p, ssd_p, zeros_state)
        y_ssd_s, _ = ssd_mixer(u_ssd, u_sm, u_smt, t_p, nb_s, l_s, ssd_p, state_ssd[:, i])
        y_gdn_p, st_gdn = gdn_mixer(u_gdn, u_sm, u_smt, 0, nb_p, l_p, gdn_p, zeros_state)
        y_gdn_s, _ = gdn_mixer(u_gdn, u_sm, u_smt, t_p, nb_s, l_s, gdn_p, state_gdn[:, i])
        y_fn_p = fnet_mixer(u_fn, 0, nb_p, l_p, dft_p)
        y_fn_s = fnet_mixer(u_fn, t_p, nb_s, l_s, dft_s)
        y_att_p, k_new, v_new = ctx_attention(u_att, 0, nb_p, l_p, att_p)
        y_att_s = latent_attention(u_att, t_p, nb_s, l_s, att_p,
                                   cache_k[:, i].reshape(nb_s, past, 2 * HEAD_DIM),
                                   cache_v[:, i].reshape(nb_s, past, 2 * HEAD_DIM), rope)
        ys = [jnp.concatenate([a, b], axis=0) for a, b in
              ((y_ssd_p, y_ssd_s), (y_gdn_p, y_gdn_s), (y_fn_p, y_fn_s), (y_att_p, y_att_s))]

        x1, h2, h2t = out_proj(ys, x, w_out[i].astype(bf16), g1, norm2_w[i][None, :], sc2, sh2, t_p, l_s)
        route = peer_route(h2, peer_w_q[i].astype(bf16), peer_keys[i])
        x = peer_experts(h2t, peer_u[i].astype(bf16), peer_v[i].astype(bf16), route, x1, g2, t_p, l_s)

        ks.append(k_new.reshape(nb_p, l_p, 2, HEAD_DIM))
        vs.append(v_new.reshape(nb_p, l_p, 2, HEAD_DIM))
        ssd_states.append(st_ssd)
        gdn_states.append(st_gdn)

    return (x[:t_p].reshape(nb_p, l_p, d), x[t_p:].reshape(nb_s, l_s, d),
            jnp.stack(ks, axis=1), jnp.stack(vs, axis=1),
            jnp.stack(ssd_states, axis=1), jnp.stack(gdn_states, axis=1))
```

```python
import functools
import math

import jax
import jax.numpy as jnp
from jax import lax
from jax.experimental import pallas as pl
from jax.experimental.pallas import tpu as pltpu

f32 = jnp.float32
bf16 = jnp.bfloat16
HI = lax.Precision.HIGHEST

D_MODEL = 1024
DEPTH = 4
EPS = 1e-6
NEG_INF = -1e30
CONV_W = 5
HEAD_DIM = 64
N_HEADS = 4
MIX = 256
SSD_CHUNK = 256
GDN_CHUNK = 64
WINDOW = 128
ABLOCK = 128
GRID_W = 64
ROPE_THETA = 10000.0
PEER_HEADS = 8
PEER_NKEYS = 128
PEER_TOPK = 16
PEER_EXPERTS = PEER_NKEYS * PEER_NKEYS

VMEM_LIMIT = 56 * 1024 * 1024

U_SSD, U_GDN, U_FN, U_ATT, U_SM = 768, 1024, 256, 512, 128
U_MAIN = U_SSD + U_GDN + U_FN + U_ATT


def _dot(a, b, prec=None):
    return jnp.dot(a, b, preferred_element_type=f32, precision=prec)


def _dot_nt(a, b, prec=None):
    return lax.dot_general(a, b, (((1,), (1,)), ((), ())), preferred_element_type=f32, precision=prec)


def _dot_tn(a, b, prec=None):
    return lax.dot_general(a, b, (((0,), (0,)), ((), ())), preferred_element_type=f32, precision=prec)


def _split_bf16(x):
    hi = x.astype(bf16)
    return hi, (x - hi.astype(f32)).astype(bf16)


def _dot3(a, b):
    ah, al = _split_bf16(a)
    bh, bl = _split_bf16(b)
    return _dot(ah, bh) + (_dot(ah, bl) + _dot(al, bh))


def _dotb(a, b):
    return _dot(a.astype(bf16), b.astype(bf16))


def _sigmoid(x):
    return 1.0 / (1.0 + jnp.exp(-x))


def _silu(x):
    return x * _sigmoid(x)


def _softplus(x):
    return jnp.maximum(x, 0.0) + jnp.log(1.0 + jnp.exp(-jnp.abs(x)))


def _params(*sem):
    return pltpu.CompilerParams(dimension_semantics=sem, vmem_limit_bytes=VMEM_LIMIT)


def _idiv(x, n):
    return lax.shift_right_logical(x, int(math.log2(n)))


def _head_mean_matrix(width):
    r = _idiv(lax.broadcasted_iota(jnp.int32, (width, width), 0), HEAD_DIM)
    c = _idiv(lax.broadcasted_iota(jnp.int32, (width, width), 1), HEAD_DIM)
    return jnp.where(r == c, 1.0 / HEAD_DIM, 0.0).astype(f32)


def _shift_rows(x, d, n):
    row = lax.broadcasted_iota(jnp.int32, x.shape, 0)
    y = pltpu.roll(x, (-d) % n, 0)
    ok = (row + d >= 0) & (row + d < n)
    return jnp.where(ok, y, 0.0)


def _conv_silu(x, w_ref, b_ref, n):
    half = (CONV_W - 1) // 2
    acc = b_ref[...] + w_ref[half:half + 1, :] * x
    for k in range(CONV_W):
        if k != half:
            acc = acc + w_ref[k:k + 1, :] * _shift_rows(x, k - half, n)
    return _silu(acc)


def _mod_body(c_ref, w_ref, b_ref, o_ref):
    o_ref[0] = _dot(_silu(c_ref[...]), w_ref[0], HI) + b_ref[0]


def adaln_all(cvecs, w_mod, b_mod):
    tn = 1536
    n = w_mod.shape[-1]
    return pl.pallas_call(
        _mod_body,
        grid=(DEPTH, n // tn),
        in_specs=[pl.BlockSpec((8, D_MODEL), lambda l, j: (0, 0)),
                  pl.BlockSpec((1, D_MODEL, tn), lambda l, j: (l, 0, j)),
                  pl.BlockSpec((1, 1, tn), lambda l, j: (l, 0, j))],
        out_specs=pl.BlockSpec((1, 8, tn), lambda l, j: (l, 0, j)),
        out_shape=jax.ShapeDtypeStruct((DEPTH, 8, n), f32),
        compiler_params=_params("arbitrary", "arbitrary"),
        name="adaln",
    )(cvecs, w_mod, b_mod.reshape(DEPTH, 1, n))


def _in_body(x_ref, nw_ref, sc_ref, sh_ref, w_ref, ws_ref, wst_ref,
             o_ssd, o_gdn, o_fn, o_att, o_sm, o_smt):
    x = x_ref[...]
    h = x * lax.rsqrt(jnp.mean(x * x, axis=-1, keepdims=True) + EPS) * nw_ref[...]
    h = h * (1.0 + sc_ref[0]) + sh_ref[0]
    u = _dot(h.astype(bf16), w_ref[...])
    o_ssd[...] = u[:, 0:U_SSD]
    o_gdn[...] = u[:, U_SSD:U_SSD + U_GDN]
    o_fn[...] = u[:, U_SSD + U_GDN:U_SSD + U_GDN + U_FN]
    o_att[...] = u[:, U_SSD + U_GDN + U_FN:U_MAIN]
    o_sm[...] = _dot(h, ws_ref[...], HI)
    o_smt[...] = _dot_nt(wst_ref[...], h, HI)


def _mod_row_map(tm, t_p, l_s):
    def f(i):
        t0 = i * tm
        return (jnp.where(t0 < t_p, 0, 1 + (t0 - t_p) // l_s), 0, 0)
    return f


def in_proj(x, nw, sc, sh, w_main, w_sm, w_smt, t_p, l_s):
    t = x.shape[0]
    tm = 512
    row = _mod_row_map(tm, t_p, l_s)
    widths = (U_SSD, U_GDN, U_FN, U_ATT, U_SM)
    return pl.pallas_call(
        _in_body,
        grid=(t // tm,),
        in_specs=[pl.BlockSpec((tm, D_MODEL), lambda i: (i, 0)),
                  pl.BlockSpec((1, D_MODEL), lambda i: (0, 0)),
                  pl.BlockSpec((1, 1, D_MODEL), row),
                  pl.BlockSpec((1, 1, D_MODEL), row),
                  pl.BlockSpec((D_MODEL, U_MAIN), lambda i: (0, 0)),
                  pl.BlockSpec((D_MODEL, U_SM), lambda i: (0, 0)),
                  pl.BlockSpec((U_SM, D_MODEL), lambda i: (0, 0))],
        out_specs=[pl.BlockSpec((tm, w), lambda i: (i, 0)) for w in widths]
        + [pl.BlockSpec((U_SM, tm), lambda i: (0, i))],
        out_shape=[jax.ShapeDtypeStruct((t, w), f32) for w in widths]
        + [jax.ShapeDtypeStruct((U_SM, t), f32)],
        compiler_params=_params("arbitrary"),
        name="in_proj",
    )(x, nw, sc, sh, w_main, w_sm, w_smt)


def _ssd_body(L, u_ref, sm_ref, smt_ref, cw_ref, cb_ref, alr_ref, alc_ref, dbr_ref, dbc_ref,
              dsk_ref, nw_ref, h0_ref, y_ref, hT_ref, yacc):
    Q = min(L, SSD_CHUNK)
    nc = L // Q
    hd = HEAD_DIM
    z = u_ref[:, 0:MIX]
    xbc = _conv_silu(u_ref[:, MIX:U_SSD], cw_ref, cb_ref, L)
    xs = xbc[:, 0:MIX]
    dtc = _softplus(sm_ref[:, 0:8] + dbr_ref[...])
    dtr = _softplus(smt_ref[0:8, :] + dbc_ref[...])
    ac = dtc * (-jnp.exp(alr_ref[...]))
    ar = dtr * (-jnp.exp(alc_ref[...]))
    ri = lax.broadcasted_iota(jnp.int32, (Q, Q), 0)
    ci = lax.broadcasted_iota(jnp.int32, (Q, Q), 1)
    low = ri >= ci
    upp = ci >= ri
    tril = jnp.where(low, 1.0, 0.0).astype(f32)

    def chunk_terms(c):
        r0 = c * Q
        a_c = ac[r0:r0 + Q, :]
        a_r = ar[:, r0:r0 + Q]
        pc = _dot(tril, a_c, HI)
        pr = _dot_nt(a_r, tril, HI)
        return a_c, a_r, pc, pr

    hf = [h0_ref[0, 0, h] for h in range(N_HEADS)]
    for c in range(nc):
        r0 = c * Q
        a_c, a_r, pc, pr = chunk_terms(c)
        ys = []
        for h in range(N_HEADS):
            g = h // 2
            x_h = xs[r0:r0 + Q, h * hd:(h + 1) * hd]
            b_g = xbc[r0:r0 + Q, MIX + g * hd:MIX + (g + 1) * hd]
            c_g = xbc[r0:r0 + Q, MIX + 2 * hd + g * hd:MIX + 2 * hd + (g + 1) * hd]
            gm = _dot_nt(c_g, b_g, HI)
            hb = N_HEADS + h
            seg_f = pc[:, h:h + 1] - pr[h:h + 1, :]
            l_f = jnp.where(low, jnp.exp(jnp.minimum(seg_f, 0.0)), 0.0)
            e_c = pc[:, hb:hb + 1] - a_c[:, hb:hb + 1]
            e_r = pr[hb:hb + 1, :] - a_r[hb:hb + 1, :]
            seg_b = e_r - e_c
            l_b = jnp.where(upp, jnp.exp(jnp.minimum(seg_b, 0.0)), 0.0)
            m = gm * (l_f * dtr[h:h + 1, r0:r0 + Q] + l_b * dtr[hb:hb + 1, r0:r0 + Q])
            y_h = _dot(m, x_h, HI)
            y_h = y_h + _dot_nt(c_g * jnp.exp(pc[:, h:h + 1]), hf[h], HI)
            tot = pc[Q - 1:Q, h:h + 1]
            wgt = dtc[r0:r0 + Q, h:h + 1] * jnp.exp(tot - pc[:, h:h + 1])
            hf[h] = hf[h] * jnp.exp(tot) + _dot_tn(x_h * wgt, b_g, HI)
            ys.append(y_h)
        yacc[r0:r0 + Q, :] = jnp.concatenate(ys, axis=1)
    hb_s = [h0_ref[0, 1, h] for h in range(N_HEADS)]
    for c in range(nc - 1, -1, -1):
        r0 = c * Q
        a_c, a_r, pc, pr = chunk_terms(c)
        ys = []
        for h in range(N_HEADS):
            g = h // 2
            hb = N_HEADS + h
            x_h = xs[r0:r0 + Q, h * hd:(h + 1) * hd]
            b_g = xbc[r0:r0 + Q, MIX + g * hd:MIX + (g + 1) * hd]
            c_g = xbc[r0:r0 + Q, MIX + 2 * hd + g * hd:MIX + 2 * hd + (g + 1) * hd]
            e_c = pc[:, hb:hb + 1] - a_c[:, hb:hb + 1]
            tot = pc[Q - 1:Q, hb:hb + 1]
            ys.append(_dot_nt(c_g * jnp.exp(tot - e_c), hb_s[h], HI))
            wgt = dtc[r0:r0 + Q, hb:hb + 1] * jnp.exp(e_c)
            hb_s[h] = hb_s[h] * jnp.exp(tot) + _dot_tn(x_h * wgt, b_g, HI)
        yacc[r0:r0 + Q, :] = yacc[r0:r0 + Q, :] + jnp.concatenate(ys, axis=1)
    for h in range(N_HEADS):
        hT_ref[0, 0, h] = hf[h]
        hT_ref[0, 1, h] = hb_s[h]
    y = yacc[...] + dsk_ref[...] * xs
    y = y * _silu(z)
    ms = _dot(y * y, _head_mean_matrix(MIX), HI)
    y_ref[...] = (y * lax.rsqrt(ms + EPS) * nw_ref[...]).astype(y_ref.dtype)


def ssd_mixer(u_ssd, u_sm, u_smt, t0, nseq, L, p, h0):
    b0 = t0 // L
    small = lambda s: pl.BlockSpec(s, lambda b: (0,) * len(s))
    return pl.pallas_call(
        functools.partial(_ssd_body, L),
        grid=(nseq,),
        in_specs=[pl.BlockSpec((L, U_SSD), lambda b: (b0 + b, 0)),
                  pl.BlockSpec((L, U_SM), lambda b: (b0 + b, 0)),
                  pl.BlockSpec((U_SM, L), lambda b: (0, b0 + b)),
                  small((CONV_W, 512)), small((1, 512)),
                  small((1, 8)), small((8, 1)), small((1, 8)), small((8, 1)),
                  small((1, MIX)), small((1, MIX)),
                  pl.BlockSpec((1, 2, N_HEADS, HEAD_DIM, HEAD_DIM), lambda b: (b, 0, 0, 0, 0))],
        out_specs=[pl.BlockSpec((L, MIX), lambda b: (b, 0)),
                   pl.BlockSpec((1, 2, N_HEADS, HEAD_DIM, HEAD_DIM), lambda b: (b, 0, 0, 0, 0))],
        out_shape=[jax.ShapeDtypeStruct((nseq * L, MIX), bf16),
                   jax.ShapeDtypeStruct((nseq, 2, N_HEADS, HEAD_DIM, HEAD_DIM), f32)],
        scratch_shapes=[pltpu.VMEM((L, MIX), f32)],
        compiler_params=_params("arbitrary"),
        name=f"ssd_{L}",
    )(u_ssd, u_sm, u_smt, p["conv_w"], p["conv_b"], p["al_row"], p["al_col"], p["db_row"], p["db_col"],
      p["d_skip"], p["norm_w"], h0)


def _gdn_body(L, u_ref, sm_ref, smt_ref, cw_ref, cb_ref, alr_ref, alc_ref, dbr_ref, dbc_ref,
              nw_ref, s0_ref, y_ref, sT_ref, qkv, oacc):
    Q = GDN_CHUNK
    nc = L // Q
    W = MIX
    qkv[...] = _conv_silu(u_ref[:, 0:3 * MIX], cw_ref, cb_ref, L)
    hm = _head_mean_matrix(MIX) * float(HEAD_DIM)
    q_all = qkv[:, 0:MIX]
    k_all = qkv[:, MIX:2 * MIX]
    qkv[:, 0:MIX] = q_all * lax.rsqrt(_dot(q_all * q_all, hm, HI) + EPS) * (HEAD_DIM ** -0.5)
    qkv[:, MIX:2 * MIX] = k_all * lax.rsqrt(_dot(k_all * k_all, hm, HI) + EPS)
    lac = -jnp.exp(alr_ref[...]) * _softplus(sm_ref[:, 8:16] + dbr_ref[...])
    lar = -jnp.exp(alc_ref[...]) * _softplus(smt_ref[8:16, :] + dbc_ref[...])
    btc = _sigmoid(sm_ref[:, 16:24])
    btr = _sigmoid(smt_ref[16:24, :])

    ri = lax.broadcasted_iota(jnp.int32, (W, W), 0)
    ci = lax.broadcasted_iota(jnp.int32, (W, W), 1)
    same_head = _idiv(ri, Q) == _idiv(ci, Q)
    blk16 = _idiv(ri, 16) == _idiv(ci, 16)
    eye = jnp.where(ri == ci, 1.0, 0.0).astype(f32)
    rq = lax.broadcasted_iota(jnp.int32, (Q, Q), 0)
    cq = lax.broadcasted_iota(jnp.int32, (Q, Q), 1)
    tril_q = jnp.where(rq >= cq, 1.0, 0.0).astype(f32)
    triu_q = jnp.where(cq >= rq, 1.0, 0.0).astype(f32)

    def stack_cols(m, base):
        return jnp.concatenate([m[:, base + h:base + h + 1] for h in range(N_HEADS)], axis=0)

    def stack_rows(m, base):
        return jnp.concatenate([m[base + h:base + h + 1, :] for h in range(N_HEADS)], axis=1)

    def block_diag(m):
        return jnp.where(same_head, jnp.concatenate([m] * N_HEADS, axis=1), 0.0)

    def heads_to_rows(m):
        return jnp.concatenate([m[:, h * HEAD_DIM:(h + 1) * HEAD_DIM] for h in range(N_HEADS)], axis=0)

    def rows_to_heads(m):
        return jnp.concatenate([m[h * Q:(h + 1) * Q, :] for h in range(N_HEADS)], axis=1)

    def sweep(d, S):
        tri_mat = tril_q if d == 0 else triu_q
        causal = (ri >= ci) if d == 0 else (ci >= ri)
        strict = (ri > ci) if d == 0 else (ci > ri)
        order = range(nc) if d == 0 else range(nc - 1, -1, -1)
        for c in order:
            r0 = c * Q
            qs = heads_to_rows(qkv[r0:r0 + Q, 0:MIX])
            ks = heads_to_rows(qkv[r0:r0 + Q, MIX:2 * MIX])
            vs = heads_to_rows(qkv[r0:r0 + Q, 2 * MIX:3 * MIX])
            la_c = lac[r0:r0 + Q, :]
            la_r = lar[:, r0:r0 + Q]
            gc = stack_cols(_dot(tri_mat, la_c, HI), 4 * d)
            gr = stack_rows(_dot_nt(la_r, tri_mat, HI), 4 * d)
            bc = stack_cols(btc[r0:r0 + Q, :], 4 * d)
            ends = []
            for h in range(N_HEADS):
                e = gr[:, h * Q + Q - 1:h * Q + Q] if d == 0 else gr[:, h * Q:h * Q + 1]
                ends.append(jnp.broadcast_to(e, (Q, 1)))
            g_end = jnp.concatenate(ends, axis=0)
            decay = jnp.where(causal & same_head, jnp.exp(jnp.minimum(gc - gr, 0.0)), 0.0)
            kbd = block_diag(ks).astype(bf16)
            kk = _dot_nt(kbd, kbd)
            a = jnp.where(strict, bc * kk * decay, 0.0)
            n = jnp.where(blk16, a, 0.0)
            n2 = _dotb(n, n)
            n4 = _dotb(n2, n2)
            n8 = _dotb(n4, n4)
            dinv = _dotb(_dotb(eye - n, eye + n2), _dotb(eye + n4, eye + n8))
            zz = _dotb(dinv, a - n)
            z2 = _dotb(zz, zz)
            tinv = _dotb(_dotb(eye - zz, eye + z2), dinv)
            rhs = jnp.concatenate([vs * bc, ks * (bc * jnp.exp(gc))], axis=1)
            sol = _dotb(tinv, rhs)
            u_c = sol[:, 0:HEAD_DIM]
            w_c = sol[:, HEAD_DIM:2 * HEAD_DIM]
            s_b = S.astype(bf16)
            attn = jnp.where(causal, _dot_nt(block_diag(qs).astype(bf16), kbd) * decay, 0.0)
            v_new = u_c - _dot(block_diag(w_c).astype(bf16), s_b)
            v_b = v_new.astype(bf16)
            o = _dot(block_diag(qs * jnp.exp(gc)).astype(bf16), s_b) + _dot(attn.astype(bf16), v_b)
            S = S * jnp.exp(g_end) + _dot_tn(block_diag(ks * jnp.exp(g_end - gc)).astype(bf16), v_b)
            o = rows_to_heads(o)
            if d == 0:
                oacc[r0:r0 + Q, :] = o
            else:
                oacc[r0:r0 + Q, :] = oacc[r0:r0 + Q, :] + o
        return S

    for d in range(2):
        S0 = jnp.concatenate([s0_ref[0, d, h] for h in range(N_HEADS)], axis=0)
        S = sweep(d, S0)
        for h in range(N_HEADS):
            sT_ref[0, d, h] = S[h * HEAD_DIM:(h + 1) * HEAD_DIM, :]
    o = oacc[...]
    ms = _dot(o * o, _head_mean_matrix(MIX), HI)
    o = o * lax.rsqrt(ms + EPS) * nw_ref[...]
    y_ref[...] = (o * _silu(u_ref[:, 3 * MIX:4 * MIX])).astype(y_ref.dtype)


def gdn_mixer(u_gdn, u_sm, u_smt, t0, nseq, L, p, s0):
    b0 = t0 // L
    small = lambda s: pl.BlockSpec(s, lambda b: (0,) * len(s))
    return pl.pallas_call(
        functools.partial(_gdn_body, L),
        grid=(nseq,),
        in_specs=[pl.BlockSpec((L, U_GDN), lambda b: (b0 + b, 0)),
                  pl.BlockSpec((L, U_SM), lambda b: (b0 + b, 0)),
                  pl.BlockSpec((U_SM, L), lambda b: (0, b0 + b)),
                  small((CONV_W, 3 * MIX)), small((1, 3 * MIX)),
                  small((1, 8)), small((8, 1)), small((1, 8)), small((8, 1)),
                  small((1, MIX)),
                  pl.BlockSpec((1, 2, N_HEADS, HEAD_DIM, HEAD_DIM), lambda b: (b, 0, 0, 0, 0))],
        out_specs=[pl.BlockSpec((L, MIX), lambda b: (b, 0)),
                   pl.BlockSpec((1, 2, N_HEADS, HEAD_DIM, HEAD_DIM), lambda b: (b, 0, 0, 0, 0))],
        out_shape=[jax.ShapeDtypeStruct((nseq * L, MIX), bf16),
                   jax.ShapeDtypeStruct((nseq, 2, N_HEADS, HEAD_DIM, HEAD_DIM), f32)],
        scratch_shapes=[pltpu.VMEM((L, 3 * MIX), f32), pltpu.VMEM((L, MIX), f32)],
        compiler_params=_params("arbitrary"),
        name=f"gdn_{L}",
    )(u_gdn, u_sm, u_smt, p["conv_w"], p["conv_b"], p["al_row"], p["al_col"], p["db_row"], p["db_col"],
      p["norm_w"], s0)


def _fnet_body(x_ref, cl_ref, sl_ref, cc_ref, sc_ref, y_ref):
    x = x_ref[...]
    xc = _dot(x, cc_ref[...], HI)
    xs = _dot(x, sc_ref[...], HI)
    y_ref[...] = (_dot(cl_ref[...], xc, HI) - _dot(sl_ref[...], xs, HI)).astype(y_ref.dtype)


def fnet_mixer(u_fn, t0, nseq, L, tabs):
    b0 = t0 // L
    full = lambda s: pl.BlockSpec(s, lambda b: (0,) * len(s))
    return pl.pallas_call(
        _fnet_body,
        grid=(nseq,),
        in_specs=[pl.BlockSpec((L, MIX), lambda b: (b0 + b, 0)),
                  full((L, L)), full((L, L)), full((MIX, MIX)), full((MIX, MIX))],
        out_specs=pl.BlockSpec((L, MIX), lambda b: (b, 0)),
        out_shape=jax.ShapeDtypeStruct((nseq * L, MIX), bf16),
        compiler_params=_params("arbitrary"),
        name=f"fnet_{L}",
    )(u_fn, *tabs)


def dft_tables(L):
    n = jnp.arange(L, dtype=jnp.int32)
    ang_l = (2.0 * math.pi / L) * ((n[:, None] * n[None, :]) % L).astype(f32)
    m = jnp.arange(MIX, dtype=jnp.int32)
    same = (m[:, None] // HEAD_DIM) == (m[None, :] // HEAD_DIM)
    ang_c = (2.0 * math.pi / HEAD_DIM) * (((m[:, None] % HEAD_DIM) * (m[None, :] % HEAD_DIM)) % HEAD_DIM).astype(f32)
    sl = 1.0 / math.sqrt(L)
    sc = 1.0 / math.sqrt(HEAD_DIM)
    return (jnp.cos(ang_l) * sl, jnp.sin(ang_l) * sl,
            jnp.where(same, jnp.cos(ang_c) * sc, 0.0), jnp.where(same, jnp.sin(ang_c) * sc, 0.0))


def _qk_norm(x, w_row, width):
    ms = _dot(x * x, _head_mean_matrix(width), HI)
    return x * lax.rsqrt(ms + EPS) * w_row


def _sink_softmax_pv(s_list, v_list, sink):
    m = sink
    for s in s_list:
        m = jnp.maximum(m, jnp.max(s, axis=-1, keepdims=True))
    den = jnp.exp(sink - m)
    acc = None
    for s, v in zip(s_list, v_list):
        e = jnp.exp(s - m)
        den = den + jnp.sum(e, axis=-1, keepdims=True)
        pv = _dot(e, v, HI)
        acc = pv if acc is None else acc + pv
    return acc / den


def _ctx_att_body(L, u_ref, qw_ref, kw_ref, sink_ref, y_ref, k_ref, v_ref):
    hd = HEAD_DIM
    scale = hd ** -0.5
    q = _qk_norm(u_ref[:, 0:MIX], qw_ref[...], MIX)
    k = _qk_norm(u_ref[:, MIX:MIX + 2 * hd], kw_ref[...], 2 * hd)
    v = u_ref[:, MIX + 2 * hd:MIX + 4 * hd]
    k_ref[...] = k
    v_ref[...] = v
    outs = []
    for g in range(2):
        k_g = k[:, g * hd:(g + 1) * hd]
        v_g = v[:, g * hd:(g + 1) * hd]
        for r in range(2):
            h = 2 * g + r
            s = _dot_nt(q[:, h * hd:(h + 1) * hd], k_g, HI) * scale
            outs.append(_sink_softmax_pv([s], [v_g], sink_ref[0:1, h:h + 1]))
    y_ref[...] = jnp.concatenate(outs, axis=1).astype(y_ref.dtype)


def ctx_attention(u_att, t0, nseq, L, p):
    b0 = t0 // L
    small = lambda s: pl.BlockSpec(s, lambda b: (0,) * len(s))
    return pl.pallas_call(
        functools.partial(_ctx_att_body, L),
        grid=(nseq,),
        in_specs=[pl.BlockSpec((L, U_ATT), lambda b: (b0 + b, 0)),
                  small((1, MIX)), small((1, 2 * HEAD_DIM)), small((1, N_HEADS))],
        out_specs=[pl.BlockSpec((L, MIX), lambda b: (b, 0)),
                   pl.BlockSpec((L, 2 * HEAD_DIM), lambda b: (b, 0)),
                   pl.BlockSpec((L, 2 * HEAD_DIM), lambda b: (b, 0))],
        out_shape=[jax.ShapeDtypeStruct((nseq * L, MIX), bf16),
                   jax.ShapeDtypeStruct((nseq * L, 2 * HEAD_DIM), f32),
                   jax.ShapeDtypeStruct((nseq * L, 2 * HEAD_DIM), f32)],
        compiler_params=_params("arbitrary"),
        name="ctx_att",
    )(u_att, p["q_norm"], p["k_norm"], p["sink"])


def _rope(x, cos_ref, sin_ref, width):
    lane = lax.broadcasted_iota(jnp.int32, x.shape, 1) & (HEAD_DIM - 1)
    half = HEAD_DIM // 2
    swapped = jnp.where(lane < half, pltpu.roll(x, width - half, 1), pltpu.roll(x, half, 1))
    return x * cos_ref[...] + swapped * sin_ref[...]


def _lat_att_body(L, P, u_ref, kc_ref, vc_ref, qw_ref, kw_ref, sink_ref, cq_ref, sq_ref, ck_ref, sk_ref, y_ref):
    hd = HEAD_DIM
    scale = hd ** -0.5
    nb = L // ABLOCK
    q = _rope(_qk_norm(u_ref[:, 0:MIX], qw_ref[...], MIX), cq_ref, sq_ref, MIX)
    k = _rope(_qk_norm(u_ref[:, MIX:MIX + 2 * hd], kw_ref[...], 2 * hd), ck_ref, sk_ref, 2 * hd)
    v = u_ref[:, MIX + 2 * hd:MIX + 4 * hd]
    kc = kc_ref[0]
    vc = vc_ref[0]
    for i in range(nb):
        lo = max(i - 1, 0) * ABLOCK
        hi = min(i + 2, nb) * ABLOCK
        qpos = i * ABLOCK + lax.broadcasted_iota(jnp.int32, (ABLOCK, hi - lo), 0)
        kpos = lo + lax.broadcasted_iota(jnp.int32, (ABLOCK, hi - lo), 1)
        dist = qpos - kpos
        ok = (dist <= WINDOW) & (dist >= -WINDOW)
        outs = []
        for g in range(2):
            k_l = k[lo:hi, g * hd:(g + 1) * hd]
            v_l = v[lo:hi, g * hd:(g + 1) * hd]
            k_c = kc[:, g * hd:(g + 1) * hd]
            v_c = vc[:, g * hd:(g + 1) * hd]
            for r in range(2):
                h = 2 * g + r
                q_h = q[i * ABLOCK:(i + 1) * ABLOCK, h * hd:(h + 1) * hd]
                s_loc = jnp.where(ok, _dot_nt(q_h, k_l, HI) * scale, NEG_INF)
                s_ctx = _dot_nt(q_h, k_c, HI) * scale
                outs.append(_sink_softmax_pv([s_loc, s_ctx], [v_l, v_c], sink_ref[0:1, h:h + 1]))
        y_ref[i * ABLOCK:(i + 1) * ABLOCK, :] = jnp.concatenate(outs, axis=1).astype(y_ref.dtype)


def latent_attention(u_att, t0, nseq, L, p, kc, vc, rope):
    b0 = t0 // L
    P = kc.shape[1]
    small = lambda s: pl.BlockSpec(s, lambda b: (0,) * len(s))
    return pl.pallas_call(
        functools.partial(_lat_att_body, L, P),
        grid=(nseq,),
        in_specs=[pl.BlockSpec((L, U_ATT), lambda b: (b0 + b, 0)),
                  pl.BlockSpec((1, P, 2 * HEAD_DIM), lambda b: (b, 0, 0)),
                  pl.BlockSpec((1, P, 2 * HEAD_DIM), lambda b: (b, 0, 0)),
                  small((1, MIX)), small((1, 2 * HEAD_DIM)), small((1, N_HEADS)),
                  small((L, MIX)), small((L, MIX)), small((L, 2 * HEAD_DIM)), small((L, 2 * HEAD_DIM))],
        out_specs=pl.BlockSpec((L, MIX), lambda b: (b, 0)),
        out_shape=jax.ShapeDtypeStruct((nseq * L, MIX), bf16),
        compiler_params=_params("arbitrary"),
        name="lat_att",
    )(u_att, kc, vc, p["q_norm"], p["k_norm"], p["sink"], *rope)


def rope_tables(L):
    rows = L // GRID_W
    row = jnp.repeat(jnp.arange(rows, dtype=f32), GRID_W)
    col = jnp.tile(jnp.arange(GRID_W, dtype=f32), rows)
    nf = HEAD_DIM // 4
    inv = ROPE_THETA ** (-jnp.arange(nf, dtype=f32) / nf)
    ang = jnp.concatenate([row[:, None] * inv, col[:, None] * inv], axis=-1)
    cos = jnp.concatenate([jnp.cos(ang), jnp.cos(ang)], axis=-1)
    sin = jnp.concatenate([-jnp.sin(ang), jnp.sin(ang)], axis=-1)
    return (jnp.tile(cos, (1, N_HEADS)), jnp.tile(sin, (1, N_HEADS)), jnp.tile(cos, (1, 2)), jnp.tile(sin, (1, 2)))


def _out_body(ya, yb, yc, yd, x_ref, w_ref, g1_ref, nw_ref, sc_ref, sh_ref, x1_ref, h2_ref, h2t_ref):
    y = _dot(ya[...], w_ref[0:MIX, :])
    y = y + _dot(yb[...], w_ref[MIX:2 * MIX, :])
    y = y + _dot(yc[...], w_ref[2 * MIX:3 * MIX, :])
    y = y + _dot(yd[...], w_ref[3 * MIX:4 * MIX, :])
    x1 = x_ref[...] + g1_ref[0] * y
    x1_ref[...] = x1
    h = x1 * lax.rsqrt(jnp.mean(x1 * x1, axis=-1, keepdims=True) + EPS) * nw_ref[...]
    h = h * (1.0 + sc_ref[0]) + sh_ref[0]
    h2_ref[...] = h.astype(bf16)
    h2t_ref[...] = h.T.astype(bf16)


def out_proj(ys, x, w_out, g1, nw, sc, sh, t_p, l_s):
    t = x.shape[0]
    tm = 512
    row = _mod_row_map(tm, t_p, l_s)
    tok = lambda w: pl.BlockSpec((tm, w), lambda i: (i, 0))
    return pl.pallas_call(
        _out_body,
        grid=(t // tm,),
        in_specs=[tok(MIX), tok(MIX), tok(MIX), tok(MIX), tok(D_MODEL),
                  pl.BlockSpec((D_MODEL, D_MODEL), lambda i: (0, 0)),
                  pl.BlockSpec((1, 1, D_MODEL), row),
                  pl.BlockSpec((1, D_MODEL), lambda i: (0, 0)),
                  pl.BlockSpec((1, 1, D_MODEL), row),
                  pl.BlockSpec((1, 1, D_MODEL), row)],
        out_specs=[tok(D_MODEL), tok(D_MODEL), pl.BlockSpec((D_MODEL, tm), lambda i: (0, i))],
        out_shape=[jax.ShapeDtypeStruct((t, D_MODEL), f32),
                   jax.ShapeDtypeStruct((t, D_MODEL), bf16),
                   jax.ShapeDtypeStruct((D_MODEL, t), bf16)],
        compiler_params=_params("arbitrary"),
        name="out_proj",
    )(*ys, x, w_out, g1, nw, sc, sh)


def _top16(s, n):
    io = lax.broadcasted_iota(jnp.int32, s.shape, 0)
    rank = jnp.full(s.shape, PEER_TOPK, jnp.int32)
    vals = []
    for r in range(PEER_TOPK):
        m = jnp.max(s, axis=0, keepdims=True)
        idx = jnp.min(jnp.where(s == m, io, n), axis=0, keepdims=True)
        hit = io == idx
        rank = jnp.where(hit, r, rank)
        s = jnp.where(hit, -jnp.inf, s)
        vals.append(m)
    return jnp.concatenate(vals, axis=0), rank


def _peer_route_body(h2_ref, wq_ref, keys_ref, r2_ref, e2_ref, lim_ref, cw_ref):
    k = PEER_TOPK
    q = _dot(h2_ref[...], wq_ref[...])
    tn = q.shape[0]
    n_b = [k // (a + 1) for a in range(k)]
    n_cand = sum(n_b)
    n_rows = -(-n_cand // 8) * 8
    ra = lax.broadcasted_iota(jnp.int32, (k, n_rows), 0)
    rc = lax.broadcasted_iota(jnp.int32, (k, n_rows), 1)
    rep_a = jnp.zeros((k, n_rows), f32)
    start = 0
    for a in range(k):
        rep_a = jnp.where((ra == a) & (rc >= start) & (rc < start + n_b[a]), 1.0, rep_a)
        start += n_b[a]
    lw = 128
    pad = jnp.full((n_rows - n_cand, lw), -jnp.inf, f32)
    for h in range(PEER_HEADS):
        s1_all = _dot_nt(keys_ref[h, 0], q[:, h * 256:h * 256 + 128], HI)
        s2_all = _dot_nt(keys_ref[h, 1], q[:, h * 256 + 128:h * 256 + 256], HI)
        for j in range(tn // lw):
            cols = slice(j * lw, (j + 1) * lw)
            s1 = s1_all[:, cols]
            s2 = s2_all[:, cols]
            t1, rank1 = _top16(s1, PEER_NKEYS)
            t2, rank2 = _top16(s2, PEER_NKEYS)
            cand = jnp.concatenate([t1[a:a + 1, :] + t2[0:n_b[a], :] for a in range(k)] + [pad], axis=0)
            best, crank = _top16(cand, n_rows)
            sel = jnp.where(crank < k, 1.0, 0.0).astype(f32)
            count_a = _dot(rep_a, sel)
            zsum = jnp.sum(jnp.exp(best - best[0:1, :]), axis=0, keepdims=True)
            lim = jnp.zeros((PEER_NKEYS, lw), f32)
            for a in range(k):
                lim = jnp.where(rank1 == a, count_a[a:a + 1, :], lim)
            r2_ref[h, :, cols] = rank2.astype(f32).astype(bf16)
            e2_ref[h, :, cols] = jnp.exp(s2 - t2[0:1, :]).astype(bf16)
            lim_ref[h, :, cols] = lim
            cw_ref[h, :, cols] = jnp.exp(s1 - t1[0:1, :]) / zsum


def peer_route(h2, w_q, keys):
    t = h2.shape[0]
    tn = 256
    arr = jax.ShapeDtypeStruct((PEER_HEADS, PEER_NKEYS, t), f32)
    arr_h = jax.ShapeDtypeStruct((PEER_HEADS, PEER_NKEYS, t), bf16)
    spec = pl.BlockSpec((PEER_HEADS, PEER_NKEYS, tn), lambda i: (0, 0, i))
    return pl.pallas_call(
        _peer_route_body,
        grid=(t // tn,),
        in_specs=[pl.BlockSpec((tn, D_MODEL), lambda i: (i, 0)),
                  pl.BlockSpec((D_MODEL, 2 * D_MODEL), lambda i: (0, 0)),
                  pl.BlockSpec((PEER_HEADS, 2, PEER_NKEYS, PEER_NKEYS), lambda i: (0, 0, 0, 0))],
        out_specs=[spec, spec, spec, spec],
        out_shape=[arr_h, arr_h, arr, arr],
        compiler_params=_params("arbitrary"),
        name="peer_route",
    )(h2, w_q, keys)


def _peer_expert_body(eb_i, h2t_ref, u_ref, v_ref, r2_ref, e2_ref, lim_ref, cw_ref, x1_ref, g2_ref,
                      o_ref, wt, acc):
    e = pl.program_id(1)

    @pl.when(e == 0)
    def _():
        acc[...] = jnp.zeros_like(acc)

    sub_i = 4
    sub = sub_i * PEER_NKEYS
    for c in range(eb_i // sub_i):
        rows = pl.ds(c * sub, sub)
        st = _dot(u_ref[rows, :], h2t_ref[...])
        for ii in range(sub_i):
            i = e * eb_i + c * sub_i + ii
            g = None
            for h in range(PEER_HEADS):
                lim = lim_ref[h, pl.ds(i, 1), :].astype(bf16)
                cw = cw_ref[h, pl.ds(i, 1), :].astype(bf16)
                term = jnp.where(r2_ref[h] < lim, e2_ref[h] * cw, jnp.zeros((), bf16))
                g = term if g is None else g + term
            s = st[ii * PEER_NKEYS:(ii + 1) * PEER_NKEYS, :]
            act = 0.5 * s * (1.0 + lax.erf(s * (2.0 ** -0.5)))
            wt[pl.ds(c * sub + ii * PEER_NKEYS, PEER_NKEYS), :] = g * act.astype(bf16)
        acc[...] += _dot_tn(wt[rows, :], v_ref[rows, :])

    @pl.when(e == pl.num_programs(1) - 1)
    def _():
        o_ref[...] = x1_ref[...] + g2_ref[0] * acc[...]


def peer_experts(h2t, u_tab, v_tab, route, x1, g2, t_p, l_s):
    t = x1.shape[0]
    tn = 512
    eb_i = 16
    eb = eb_i * PEER_NKEYS
    row = _mod_row_map(tn, t_p, l_s)
    rspec = pl.BlockSpec((PEER_HEADS, PEER_NKEYS, tn), lambda j, e: (0, 0, j))
    return pl.pallas_call(
        functools.partial(_peer_expert_body, eb_i),
        grid=(t // tn, PEER_EXPERTS // eb),
        in_specs=[pl.BlockSpec((D_MODEL, tn), lambda j, e: (0, j)),
                  pl.BlockSpec((eb, D_MODEL), lambda j, e: (e, 0)),
                  pl.BlockSpec((eb, D_MODEL), lambda j, e: (e, 0)),
                  rspec, rspec, rspec, rspec,
                  pl.BlockSpec((tn, D_MODEL), lambda j, e: (j, 0)),
                  pl.BlockSpec((1, 1, D_MODEL), lambda j, e: row(j))],
        out_specs=pl.BlockSpec((tn, D_MODEL), lambda j, e: (j, 0)),
        out_shape=jax.ShapeDtypeStruct((t, D_MODEL), f32),
        scratch_shapes=[pltpu.VMEM((eb, tn), bf16), pltpu.VMEM((tn, D_MODEL), f32)],
        compiler_params=_params("arbitrary", "arbitrary"),
        name="peer_experts",
    )(h2t, u_tab, v_tab, *route, x1, g2)


def _regroup_w_in(w):
    o_gdn = 776
    o_fn = o_gdn + 1040
    o_att = o_fn + 256
    main = jnp.concatenate([w[:, 0:768], w[:, o_gdn:o_gdn + 1024], w[:, o_fn:o_fn + 256], w[:, o_att:o_att + 512]], axis=1)
    small = jnp.concatenate([w[:, 768:776], w[:, o_gdn + 1024:o_gdn + 1040],
                             jnp.zeros((w.shape[0], U_SM - 24), w.dtype)], axis=1)
    return main.astype(bf16), small, small.T


def _layer_params(i, ssd_conv_w, ssd_conv_b, ssd_A_log, ssd_dt_bias, ssd_D, ssd_norm_w,
                  gdn_conv_w, gdn_conv_b, gdn_A_log, gdn_dt_bias, gdn_norm_w, q_norm_w, k_norm_w, att_sink):
    ssd = dict(conv_w=ssd_conv_w[i], conv_b=ssd_conv_b[i][None, :],
               al_row=ssd_A_log[i].reshape(1, 8), al_col=ssd_A_log[i].reshape(8, 1),
               db_row=ssd_dt_bias[i].reshape(1, 8), db_col=ssd_dt_bias[i].reshape(8, 1),
               d_skip=jnp.repeat(ssd_D[i], HEAD_DIM)[None, :], norm_w=ssd_norm_w[i][None, :])
    gdn = dict(conv_w=gdn_conv_w[i], conv_b=gdn_conv_b[i][None, :],
               al_row=gdn_A_log[i].reshape(1, 8), al_col=gdn_A_log[i].reshape(8, 1),
               db_row=gdn_dt_bias[i].reshape(1, 8), db_col=gdn_dt_bias[i].reshape(8, 1),
               norm_w=jnp.tile(gdn_norm_w[i], N_HEADS)[None, :])
    att = dict(q_norm=jnp.tile(q_norm_w[i], N_HEADS)[None, :], k_norm=jnp.tile(k_norm_w[i], 2)[None, :],
               sink=att_sink[i][None, :])
    return ssd, gdn, att


def kernel(x_prompt, x_sample, cache_k, cache_v, state_ssd, state_gdn, c, c_ctx, w_mod, b_mod, norm1_w, norm2_w, w_in, w_out, ssd_conv_w, ssd_conv_b, ssd_A_log, ssd_dt_bias, ssd_D, ssd_norm_w, gdn_conv_w, gdn_conv_b, gdn_A_log, gdn_dt_bias, gdn_norm_w, q_norm_w, k_norm_w, att_sink, peer_w_q, peer_keys, peer_u, peer_v):
    nb_p, l_p, d = x_prompt.shape
    nb_s, l_s, _ = x_sample.shape
    past = cache_k.shape[2]
    t_p = nb_p * l_p
    x = jnp.concatenate([x_prompt.reshape(t_p, d), x_sample.reshape(nb_s * l_s, d)], axis=0)

    cvecs = jnp.concatenate([c_ctx[None, :], c, jnp.zeros((8 - 1 - nb_s, d), f32)], axis=0)
    mods = adaln_all(cvecs, w_mod, b_mod)
    dft_p, dft_s = dft_tables(l_p), dft_tables(l_s)
    rope = rope_tables(l_s)
    zeros_state = jnp.zeros((nb_p, 2, N_HEADS, HEAD_DIM, HEAD_DIM), f32)

    ks, vs, ssd_states, gdn_states = [], [], [], []
    for i in range(DEPTH):
        sh1, sc1, g1, sh2, sc2, g2 = [mods[i, :, j * d:(j + 1) * d].reshape(8, 1, d) for j in range(6)]
        ssd_p, gdn_p, att_p = _layer_params(i, ssd_conv_w, ssd_conv_b, ssd_A_log, ssd_dt_bias, ssd_D, ssd_norm_w,
                                            gdn_conv_w, gdn_conv_b, gdn_A_log, gdn_dt_bias, gdn_norm_w,
                                            q_norm_w, k_norm_w, att_sink)
        w_main, w_sm, w_smt = _regroup_w_in(w_in[i])
        u_ssd, u_gdn, u_fn, u_att, u_sm, u_smt = in_proj(x, norm1_w[i][None, :], sc1, sh1, w_main, w_sm, w_smt, t_p, l_s)

        y_ssd_p, st_ssd = ssd_mixer(u_ssd, u_sm, u_smt, 0, nb_p, l_p, ssd_p, zeros_state)
        y_ssd_s, _ = ssd_mixer(u_ssd, u_sm, u_smt, t_p, nb_s, l_s, ssd_p, state_ssd[:, i])
        y_gdn_p, st_gdn = gdn_mixer(u_gdn, u_sm, u_smt, 0, nb_p, l_p, gdn_p, zeros_state)
        y_gdn_s, _ = gdn_mixer(u_gdn, u_sm, u_smt, t_p, nb_s, l_s, gdn_p, state_gdn[:, i])
        y_fn_p = fnet_mixer(u_fn, 0, nb_p, l_p, dft_p)
        y_fn_s = fnet_mixer(u_fn, t_p, nb_s, l_s, dft_s)
        y_att_p, k_new, v_new = ctx_attention(u_att, 0, nb_p, l_p, att_p)
        y_att_s = latent_attention(u_att, t_p, nb_s, l_s, att_p,
                                   cache_k[:, i].reshape(nb_s, past, 2 * HEAD_DIM),
                                   cache_v[:, i].reshape(nb_s, past, 2 * HEAD_DIM), rope)
        ys = [jnp.concatenate([a, b], axis=0) for a, b in
              ((y_ssd_p, y_ssd_s), (y_gdn_p, y_gdn_s), (y_fn_p, y_fn_s), (y_att_p, y_att_s))]

        x1, h2, h2t = out_proj(ys, x, w_out[i].astype(bf16), g1, norm2_w[i][None, :], sc2, sh2, t_p, l_s)
        route = peer_route(h2, peer_w_q[i].astype(bf16), peer_keys[i])
        x = peer_experts(h2t, peer_u[i].astype(bf16), peer_v[i].astype(bf16), route, x1, g2, t_p, l_s)

        ks.append(k_new.reshape(nb_p, l_p, 2, HEAD_DIM))
        vs.append(v_new.reshape(nb_p, l_p, 2, HEAD_DIM))
        ssd_states.append(st_ssd)
        gdn_states.append(st_gdn)

    return (x[:t_p].reshape(nb_p, l_p, d), x[t_p:].reshape(nb_s, l_s, d),
            jnp.stack(ks, axis=1), jnp.stack(vs, axis=1),
            jnp.stack(ssd_states, axis=1), jnp.stack(gdn_states, axis=1))
```

```python
import functools
import math

import jax
import jax.numpy as jnp
from jax import lax
from jax.experimental import pallas as pl
from jax.experimental.pallas import tpu as pltpu

f32 = jnp.float32
bf16 = jnp.bfloat16
HI = lax.Precision.HIGHEST

D_MODEL = 1024
DEPTH = 4
EPS = 1e-6
NEG_INF = -1e30
CONV_W = 5
HEAD_DIM = 64
N_HEADS = 4
MIX = 256
SSD_CHUNK = 256
GDN_CHUNK = 64
WINDOW = 128
ABLOCK = 128
GRID_W = 64
ROPE_THETA = 10000.0
PEER_HEADS = 8
PEER_NKEYS = 128
PEER_TOPK = 16
PEER_EXPERTS = PEER_NKEYS * PEER_NKEYS

VMEM_LIMIT = 56 * 1024 * 1024

U_SSD, U_GDN, U_FN, U_ATT, U_SM = 768, 1024, 256, 512, 128
U_MAIN = U_SSD + U_GDN + U_FN + U_ATT


def _dot(a, b, prec=None):
    return jnp.dot(a, b, preferred_element_type=f32, precision=prec)


def _dot_nt(a, b, prec=None):
    return lax.dot_general(a, b, (((1,), (1,)), ((), ())), preferred_element_type=f32, precision=prec)


def _dot_tn(a, b, prec=None):
    return lax.dot_general(a, b, (((0,), (0,)), ((), ())), preferred_element_type=f32, precision=prec)


def _split_bf16(x):
    hi = x.astype(bf16)
    return hi, (x - hi.astype(f32)).astype(bf16)


def _dot3(a, b):
    ah, al = _split_bf16(a)
    bh, bl = _split_bf16(b)
    return _dot(ah, bh) + (_dot(ah, bl) + _dot(al, bh))


def _dotb(a, b):
    return _dot(a.astype(bf16), b.astype(bf16))


def _sigmoid(x):
    return 1.0 / (1.0 + jnp.exp(-x))


def _silu(x):
    return x * _sigmoid(x)


def _softplus(x):
    return jnp.maximum(x, 0.0) + jnp.log(1.0 + jnp.exp(-jnp.abs(x)))


def _params(*sem):
    return pltpu.CompilerParams(dimension_semantics=sem, vmem_limit_bytes=VMEM_LIMIT)


def _idiv(x, n):
    return lax.shift_right_logical(x, int(math.log2(n)))


def _head_mean_matrix(width):
    r = _idiv(lax.broadcasted_iota(jnp.int32, (width, width), 0), HEAD_DIM)
    c = _idiv(lax.broadcasted_iota(jnp.int32, (width, width), 1), HEAD_DIM)
    return jnp.where(r == c, 1.0 / HEAD_DIM, 0.0).astype(f32)


def _shift_rows(x, d, n):
    row = lax.broadcasted_iota(jnp.int32, x.shape, 0)
    y = pltpu.roll(x, (-d) % n, 0)
    ok = (row + d >= 0) & (row + d < n)
    return jnp.where(ok, y, 0.0)


def _conv_silu(x, w_ref, b_ref, n):
    half = (CONV_W - 1) // 2
    acc = b_ref[...] + w_ref[half:half + 1, :] * x
    for k in range(CONV_W):
        if k != half:
            acc = acc + w_ref[k:k + 1, :] * _shift_rows(x, k - half, n)
    return _silu(acc)


def _mod_body(c_ref, w_ref, b_ref, o_ref):
    o_ref[0] = _dot(_silu(c_ref[...]), w_ref[0], HI) + b_ref[0]


def adaln_all(cvecs, w_mod, b_mod):
    tn = 1536
    n = w_mod.shape[-1]
    return pl.pallas_call(
        _mod_body,
        grid=(DEPTH, n // tn),
        in_specs=[pl.BlockSpec((8, D_MODEL), lambda l, j: (0, 0)),
                  pl.BlockSpec((1, D_MODEL, tn), lambda l, j: (l, 0, j)),
                  pl.BlockSpec((1, 1, tn), lambda l, j: (l, 0, j))],
        out_specs=pl.BlockSpec((1, 8, tn), lambda l, j: (l, 0, j)),
        out_shape=jax.ShapeDtypeStruct((DEPTH, 8, n), f32),
        compiler_params=_params("arbitrary", "arbitrary"),
        name="adaln",
    )(cvecs, w_mod, b_mod.reshape(DEPTH, 1, n))


def _in_body(x_ref, nw_ref, sc_ref, sh_ref, w_ref, ws_ref, wst_ref,
             o_ssd, o_gdn, o_fn, o_att, o_sm, o_smt):
    x = x_ref[...]
    h = x * lax.rsqrt(jnp.mean(x * x, axis=-1, keepdims=True) + EPS) * nw_ref[...]
    h = h * (1.0 + sc_ref[0]) + sh_ref[0]
    u = _dot(h.astype(bf16), w_ref[...])
    o_ssd[...] = u[:, 0:U_SSD]
    o_gdn[...] = u[:, U_SSD:U_SSD + U_GDN]
    o_fn[...] = u[:, U_SSD + U_GDN:U_SSD + U_GDN + U_FN]
    o_att[...] = u[:, U_SSD + U_GDN + U_FN:U_MAIN]
    o_sm[...] = _dot(h, ws_ref[...], HI)
    o_smt[...] = _dot_nt(wst_ref[...], h, HI)


def _mod_row_map(tm, t_p, l_s):
    def f(i):
        t0 = i * tm
        return (jnp.where(t0 < t_p, 0, 1 + (t0 - t_p) // l_s), 0, 0)
    return f


def in_proj(x, nw, sc, sh, w_main, w_sm, w_smt, t_p, l_s):
    t = x.shape[0]
    tm = 512
    row = _mod_row_map(tm, t_p, l_s)
    widths = (U_SSD, U_GDN, U_FN, U_ATT, U_SM)
    return pl.pallas_call(
        _in_body,
        grid=(t // tm,),
        in_specs=[pl.BlockSpec((tm, D_MODEL), lambda i: (i, 0)),
                  pl.BlockSpec((1, D_MODEL), lambda i: (0, 0)),
                  pl.BlockSpec((1, 1, D_MODEL), row),
                  pl.BlockSpec((1, 1, D_MODEL), row),
                  pl.BlockSpec((D_MODEL, U_MAIN), lambda i: (0, 0)),
                  pl.BlockSpec((D_MODEL, U_SM), lambda i: (0, 0)),
                  pl.BlockSpec((U_SM, D_MODEL), lambda i: (0, 0))],
        out_specs=[pl.BlockSpec((tm, w), lambda i: (i, 0)) for w in widths]
        + [pl.BlockSpec((U_SM, tm), lambda i: (0, i))],
        out_shape=[jax.ShapeDtypeStruct((t, w), f32) for w in widths]
        + [jax.ShapeDtypeStruct((U_SM, t), f32)],
        compiler_params=_params("arbitrary"),
        name="in_proj",
    )(x, nw, sc, sh, w_main, w_sm, w_smt)


def _ssd_body(L, u_ref, sm_ref, smt_ref, cw_ref, cb_ref, alr_ref, alc_ref, dbr_ref, dbc_ref,
              dsk_ref, nw_ref, h0_ref, y_ref, hT_ref, yacc):
    Q = min(L, SSD_CHUNK)
    nc = L // Q
    hd = HEAD_DIM
    z = u_ref[:, 0:MIX]
    xbc = _conv_silu(u_ref[:, MIX:U_SSD], cw_ref, cb_ref, L)
    xs = xbc[:, 0:MIX]
    dtc = _softplus(sm_ref[:, 0:8] + dbr_ref[...])
    dtr = _softplus(smt_ref[0:8, :] + dbc_ref[...])
    ac = dtc * (-jnp.exp(alr_ref[...]))
    ar = dtr * (-jnp.exp(alc_ref[...]))
    ri = lax.broadcasted_iota(jnp.int32, (Q, Q), 0)
    ci = lax.broadcasted_iota(jnp.int32, (Q, Q), 1)
    low = ri >= ci
    upp = ci >= ri
    tril = jnp.where(low, 1.0, 0.0).astype(f32)

    def chunk_terms(c):
        r0 = c * Q
        a_c = ac[r0:r0 + Q, :]
        a_r = ar[:, r0:r0 + Q]
        pc = _dot(tril, a_c, HI)
        pr = _dot_nt(a_r, tril, HI)
        return a_c, a_r, pc, pr

    hf = [h0_ref[0, 0, h] for h in range(N_HEADS)]
    for c in range(nc):
        r0 = c * Q
        a_c, a_r, pc, pr = chunk_terms(c)
        ys = []
        for h in range(N_HEADS):
            g = h // 2
            x_h = xs[r0:r0 + Q, h * hd:(h + 1) * hd]
            b_g = xbc[r0:r0 + Q, MIX + g * hd:MIX + (g + 1) * hd]
            c_g = xbc[r0:r0 + Q, MIX + 2 * hd + g * hd:MIX + 2 * hd + (g + 1) * hd]
            gm = _dot_nt(c_g, b_g, HI)
            hb = N_HEADS + h
            seg_f = pc[:, h:h + 1] - pr[h:h + 1, :]
            l_f = jnp.where(low, jnp.exp(jnp.minimum(seg_f, 0.0)), 0.0)
            e_c = pc[:, hb:hb + 1] - a_c[:, hb:hb + 1]
            e_r = pr[hb:hb + 1, :] - a_r[hb:hb + 1, :]
            seg_b = e_r - e_c
            l_b = jnp.where(upp, jnp.exp(jnp.minimum(seg_b, 0.0)), 0.0)
            m = gm * (l_f * dtr[h:h + 1, r0:r0 + Q] + l_b * dtr[hb:hb + 1, r0:r0 + Q])
            y_h = _dot(m, x_h, HI)
            y_h = y_h + _dot_nt(c_g * jnp.exp(pc[:, h:h + 1]), hf[h], HI)
            tot = pc[Q - 1:Q, h:h + 1]
            wgt = dtc[r0:r0 + Q, h:h + 1] * jnp.exp(tot - pc[:, h:h + 1])
            hf[h] = hf[h] * jnp.exp(tot) + _dot_tn(x_h * wgt, b_g, HI)
            ys.append(y_h)
        yacc[r0:r0 + Q, :] = jnp.concatenate(ys, axis=1)
    hb_s = [h0_ref[0, 1, h] for h in range(N_HEADS)]
    for c in range(nc - 1, -1, -1):
        r0 = c * Q
        a_c, a_r, pc, pr = chunk_terms(c)
        ys = []
        for h in range(N_HEADS):
            g = h // 2
            hb = N_HEADS + h
            x_h = xs[r0:r0 + Q, h * hd:(h + 1) * hd]
            b_g = xbc[r0:r0 + Q, MIX + g * hd:MIX + (g + 1) * hd]
            c_g = xbc[r0:r0 + Q, MIX + 2 * hd + g * hd:MIX + 2 * hd + (g + 1) * hd]
            e_c = pc[:, hb:hb + 1] - a_c[:, hb:hb + 1]
            tot = pc[Q - 1:Q, hb:hb + 1]
            ys.append(_dot_nt(c_g * jnp.exp(tot - e_c), hb_s[h], HI))
            wgt = dtc[r0:r0 + Q, hb:hb + 1] * jnp.exp(e_c)
            hb_s[h] = hb_s[h] * jnp.exp(tot) + _dot_tn(x_h * wgt, b_g, HI)
        yacc[r0:r0 + Q, :] = yacc[r0:r0 + Q, :] + jnp.concatenate(ys, axis=1)
    for h in range(N_HEADS):
        hT_ref[0, 0, h] = hf[h]
        hT_ref[0, 1, h] = hb_s[h]
    y = yacc[...] + dsk_ref[...] * xs
    y = y * _silu(z)
    ms = _dot(y * y, _head_mean_matrix(MIX), HI)
    y_ref[...] = (y * lax.rsqrt(ms + EPS) * nw_ref[...]).astype(y_ref.dtype)


def ssd_mixer(u_ssd, u_sm, u_smt, t0, nseq, L, p, h0):
    b0 = t0 // L
    small = lambda s: pl.BlockSpec(s, lambda b: (0,) * len(s))
    return pl.pallas_call(
        functools.partial(_ssd_body, L),
        grid=(nseq,),
        in_specs=[pl.BlockSpec((L, U_SSD), lambda b: (b0 + b, 0)),
                  pl.BlockSpec((L, U_SM), lambda b: (b0 + b, 0)),
                  pl.BlockSpec((U_SM, L), lambda b: (0, b0 + b)),
                  small((CONV_W, 512)), small((1, 512)),
                  small((1, 8)), small((8, 1)), small((1, 8)), small((8, 1)),
                  small((1, MIX)), small((1, MIX)),
                  pl.BlockSpec((1, 2, N_HEADS, HEAD_DIM, HEAD_DIM), lambda b: (b, 0, 0, 0, 0))],
        out_specs=[pl.BlockSpec((L, MIX), lambda b: (b, 0)),
                   pl.BlockSpec((1, 2, N_HEADS, HEAD_DIM, HEAD_DIM), lambda b: (b, 0, 0, 0, 0))],
        out_shape=[jax.ShapeDtypeStruct((nseq * L, MIX), bf16),
                   jax.ShapeDtypeStruct((nseq, 2, N_HEADS, HEAD_DIM, HEAD_DIM), f32)],
        scratch_shapes=[pltpu.VMEM((L, MIX), f32)],
        compiler_params=_params("arbitrary"),
        name=f"ssd_{L}",
    )(u_ssd, u_sm, u_smt, p["conv_w"], p["conv_b"], p["al_row"], p["al_col"], p["db_row"], p["db_col"],
      p["d_skip"], p["norm_w"], h0)


def _gdn_body(L, u_ref, sm_ref, smt_ref, cw_ref, cb_ref, alr_ref, alc_ref, dbr_ref, dbc_ref,
              nw_ref, s0_ref, y_ref, sT_ref, qkv, oacc):
    Q = GDN_CHUNK
    nc = L // Q
    W = MIX
    qkv[...] = _conv_silu(u_ref[:, 0:3 * MIX], cw_ref, cb_ref, L)
    hm = _head_mean_matrix(MIX) * float(HEAD_DIM)
    q_all = qkv[:, 0:MIX]
    k_all = qkv[:, MIX:2 * MIX]
    qkv[:, 0:MIX] = q_all * lax.rsqrt(_dot(q_all * q_all, hm, HI) + EPS) * (HEAD_DIM ** -0.5)
    qkv[:, MIX:2 * MIX] = k_all * lax.rsqrt(_dot(k_all * k_all, hm, HI) + EPS)
    lac = -jnp.exp(alr_ref[...]) * _softplus(sm_ref[:, 8:16] + dbr_ref[...])
    lar = -jnp.exp(alc_ref[...]) * _softplus(smt_ref[8:16, :] + dbc_ref[...])
    btc = _sigmoid(sm_ref[:, 16:24])
    btr = _sigmoid(smt_ref[16:24, :])

    ri = lax.broadcasted_iota(jnp.int32, (W, W), 0)
    ci = lax.broadcasted_iota(jnp.int32, (W, W), 1)
    same_head = _idiv(ri, Q) == _idiv(ci, Q)
    blk16 = _idiv(ri, 16) == _idiv(ci, 16)
    eye = jnp.where(ri == ci, 1.0, 0.0).astype(f32)
    rq = lax.broadcasted_iota(jnp.int32, (Q, Q), 0)
    cq = lax.broadcasted_iota(jnp.int32, (Q, Q), 1)
    tril_q = jnp.where(rq >= cq, 1.0, 0.0).astype(f32)
    triu_q = jnp.where(cq >= rq, 1.0, 0.0).astype(f32)

    def stack_cols(m, base):
        return jnp.concatenate([m[:, base + h:base + h + 1] for h in range(N_HEADS)], axis=0)

    def stack_rows(m, base):
        return jnp.concatenate([m[base + h:base + h + 1, :] for h in range(N_HEADS)], axis=1)

    def block_diag(m):
        return jnp.where(same_head, jnp.concatenate([m] * N_HEADS, axis=1), 0.0)

    def heads_to_rows(m):
        return jnp.concatenate([m[:, h * HEAD_DIM:(h + 1) * HEAD_DIM] for h in range(N_HEADS)], axis=0)

    def rows_to_heads(m):
        return jnp.concatenate([m[h * Q:(h + 1) * Q, :] for h in range(N_HEADS)], axis=1)

    def sweep(d, S):
        tri_mat = tril_q if d == 0 else triu_q
        causal = (ri >= ci) if d == 0 else (ci >= ri)
        strict = (ri > ci) if d == 0 else (ci > ri)
        order = range(nc) if d == 0 else range(nc - 1, -1, -1)
        for c in order:
            r0 = c * Q
            qs = heads_to_rows(qkv[r0:r0 + Q, 0:MIX])
            ks = heads_to_rows(qkv[r0:r0 + Q, MIX:2 * MIX])
            vs = heads_to_rows(qkv[r0:r0 + Q, 2 * MIX:3 * MIX])
            la_c = lac[r0:r0 + Q, :]
            la_r = lar[:, r0:r0 + Q]
            gc = stack_cols(_dot(tri_mat, la_c, HI), 4 * d)
            gr = stack_rows(_dot_nt(la_r, tri_mat, HI), 4 * d)
            bc = stack_cols(btc[r0:r0 + Q, :], 4 * d)
            ends = []
            for h in range(N_HEADS):
                e = gr[:, h * Q + Q - 1:h * Q + Q] if d == 0 else gr[:, h * Q:h * Q + 1]
                ends.append(jnp.broadcast_to(e, (Q, 1)))
            g_end = jnp.concatenate(ends, axis=0)
            decay = jnp.where(causal & same_head, jnp.exp(jnp.minimum(gc - gr, 0.0)), 0.0)
            kbd = block_diag(ks).astype(bf16)
            kk = _dot_nt(kbd, kbd)
            a = jnp.where(strict, bc * kk * decay, 0.0)
            n = jnp.where(blk16, a, 0.0)
            n2 = _dotb(n, n)
            n4 = _dotb(n2, n2)
            n8 = _dotb(n4, n4)
            dinv = _dotb(_dotb(eye - n, eye + n2), _dotb(eye + n4, eye + n8))
            zz = _dotb(dinv, a - n)
            z2 = _dotb(zz, zz)
            tinv = _dotb(_dotb(eye - zz, eye + z2), dinv)
            rhs = jnp.concatenate([vs * bc, ks * (bc * jnp.exp(gc))], axis=1)
            sol = _dotb(tinv, rhs)
            u_c = sol[:, 0:HEAD_DIM]
            w_c = sol[:, HEAD_DIM:2 * HEAD_DIM]
            s_b = S.astype(bf16)
            attn = jnp.where(causal, _dot_nt(block_diag(qs).astype(bf16), kbd) * decay, 0.0)
            v_new = u_c - _dot(block_diag(w_c).astype(bf16), s_b)
            v_b = v_new.astype(bf16)
            o = _dot(block_diag(qs * jnp.exp(gc)).astype(bf16), s_b) + _dot(attn.astype(bf16), v_b)
            S = S * jnp.exp(g_end) + _dot_tn(block_diag(ks * jnp.exp(g_end - gc)).astype(bf16), v_b)
            o = rows_to_heads(o)
            if d == 0:
                oacc[r0:r0 + Q, :] = o
            else:
                oacc[r0:r0 + Q, :] = oacc[r0:r0 + Q, :] + o
        return S

    for d in range(2):
        S0 = jnp.concatenate([s0_ref[0, d, h] for h in range(N_HEADS)], axis=0)
        S = sweep(d, S0)
        for h in range(N_HEADS):
            sT_ref[0, d, h] = S[h * HEAD_DIM:(h + 1) * HEAD_DIM, :]
    o = oacc[...]
    ms = _dot(o * o, _head_mean_matrix(MIX), HI)
    o = o * lax.rsqrt(ms + EPS) * nw_ref[...]
    y_ref[...] = (o * _silu(u_ref[:, 3 * MIX:4 * MIX])).astype(y_ref.dtype)


def gdn_mixer(u_gdn, u_sm, u_smt, t0, nseq, L, p, s0):
    b0 = t0 // L
    small = lambda s: pl.BlockSpec(s, lambda b: (0,) * len(s))
    return pl.pallas_call(
        functools.partial(_gdn_body, L),
        grid=(nseq,),
        in_specs=[pl.BlockSpec((L, U_GDN), lambda b: (b0 + b, 0)),
                  pl.BlockSpec((L, U_SM), lambda b: (b0 + b, 0)),
                  pl.BlockSpec((U_SM, L), lambda b: (0, b0 + b)),
                  small((CONV_W, 3 * MIX)), small((1, 3 * MIX)),
                  small((1, 8)), small((8, 1)), small((1, 8)), small((8, 1)),
                  small((1, MIX)),
                  pl.BlockSpec((1, 2, N_HEADS, HEAD_DIM, HEAD_DIM), lambda b: (b, 0, 0, 0, 0))],
        out_specs=[pl.BlockSpec((L, MIX), lambda b: (b, 0)),
                   pl.BlockSpec((1, 2, N_HEADS, HEAD_DIM, HEAD_DIM), lambda b: (b, 0, 0, 0, 0))],
        out_shape=[jax.ShapeDtypeStruct((nseq * L, MIX), bf16),
                   jax.ShapeDtypeStruct((nseq, 2, N_HEADS, HEAD_DIM, HEAD_DIM), f32)],
        scratch_shapes=[pltpu.VMEM((L, 3 * MIX), f32), pltpu.VMEM((L, MIX), f32)],
        compiler_params=_params("arbitrary"),
        name=f"gdn_{L}",
    )(u_gdn, u_sm, u_smt, p["conv_w"], p["conv_b"], p["al_row"], p["al_col"], p["db_row"], p["db_col"],
      p["norm_w"], s0)


def _fnet_body(x_ref, cl_ref, sl_ref, cc_ref, sc_ref, y_ref):
    x = x_ref[...]
    xc = _dot(x, cc_ref[...], HI)
    xs = _dot(x, sc_ref[...], HI)
    y_ref[...] = (_dot(cl_ref[...], xc, HI) - _dot(sl_ref[...], xs, HI)).astype(y_ref.dtype)


def fnet_mixer(u_fn, t0, nseq, L, tabs):
    b0 = t0 // L
    full = lambda s: pl.BlockSpec(s, lambda b: (0,) * len(s))
    return pl.pallas_call(
        _fnet_body,
        grid=(nseq,),
        in_specs=[pl.BlockSpec((L, MIX), lambda b: (b0 + b, 0)),
                  full((L, L)), full((L, L)), full((MIX, MIX)), full((MIX, MIX))],
        out_specs=pl.BlockSpec((L, MIX), lambda b: (b, 0)),
        out_shape=jax.ShapeDtypeStruct((nseq * L, MIX), bf16),
        compiler_params=_params("arbitrary"),
        name=f"fnet_{L}",
    )(u_fn, *tabs)


def dft_tables(L):
    n = jnp.arange(L, dtype=jnp.int32)
    ang_l = (2.0 * math.pi / L) * ((n[:, None] * n[None, :]) % L).astype(f32)
    m = jnp.arange(MIX, dtype=jnp.int32)
    same = (m[:, None] // HEAD_DIM) == (m[None, :] // HEAD_DIM)
    ang_c = (2.0 * math.pi / HEAD_DIM) * (((m[:, None] % HEAD_DIM) * (m[None, :] % HEAD_DIM)) % HEAD_DIM).astype(f32)
    sl = 1.0 / math.sqrt(L)
    sc = 1.0 / math.sqrt(HEAD_DIM)
    return (jnp.cos(ang_l) * sl, jnp.sin(ang_l) * sl,
            jnp.where(same, jnp.cos(ang_c) * sc, 0.0), jnp.where(same, jnp.sin(ang_c) * sc, 0.0))


def _qk_norm(x, w_row, width):
    ms = _dot(x * x, _head_mean_matrix(width), HI)
    return x * lax.rsqrt(ms + EPS) * w_row


def _sink_softmax_pv(s_list, v_list, sink):
    m = sink
    for s in s_list:
        m = jnp.maximum(m, jnp.max(s, axis=-1, keepdims=True))
    den = jnp.exp(sink - m)
    acc = None
    for s, v in zip(s_list, v_list):
        e = jnp.exp(s - m)
        den = den + jnp.sum(e, axis=-1, keepdims=True)
        pv = _dotb(e, v)
        acc = pv if acc is None else acc + pv
    return acc / den


def _ctx_att_body(L, u_ref, qw_ref, kw_ref, sink_ref, y_ref, k_ref, v_ref):
    hd = HEAD_DIM
    scale = hd ** -0.5
    q = _qk_norm(u_ref[:, 0:MIX], qw_ref[...], MIX)
    k = _qk_norm(u_ref[:, MIX:MIX + 2 * hd], kw_ref[...], 2 * hd)
    v = u_ref[:, MIX + 2 * hd:MIX + 4 * hd]
    k_ref[...] = k
    v_ref[...] = v
    outs = []
    for g in range(2):
        k_g = k[:, g * hd:(g + 1) * hd]
        v_g = v[:, g * hd:(g + 1) * hd]
        for r in range(2):
            h = 2 * g + r
            s = _dot_nt(q[:, h * hd:(h + 1) * hd].astype(bf16), k_g.astype(bf16)) * scale
            outs.append(_sink_softmax_pv([s], [v_g], sink_ref[0:1, h:h + 1]))
    y_ref[...] = jnp.concatenate(outs, axis=1).astype(y_ref.dtype)


def ctx_attention(u_att, t0, nseq, L, p):
    b0 = t0 // L
    small = lambda s: pl.BlockSpec(s, lambda b: (0,) * len(s))
    return pl.pallas_call(
        functools.partial(_ctx_att_body, L),
        grid=(nseq,),
        in_specs=[pl.BlockSpec((L, U_ATT), lambda b: (b0 + b, 0)),
                  small((1, MIX)), small((1, 2 * HEAD_DIM)), small((1, N_HEADS))],
        out_specs=[pl.BlockSpec((L, MIX), lambda b: (b, 0)),
                   pl.BlockSpec((L, 2 * HEAD_DIM), lambda b: (b, 0)),
                   pl.BlockSpec((L, 2 * HEAD_DIM), lambda b: (b, 0))],
        out_shape=[jax.ShapeDtypeStruct((nseq * L, MIX), bf16),
                   jax.ShapeDtypeStruct((nseq * L, 2 * HEAD_DIM), f32),
                   jax.ShapeDtypeStruct((nseq * L, 2 * HEAD_DIM), f32)],
        compiler_params=_params("arbitrary"),
        name="ctx_att",
    )(u_att, p["q_norm"], p["k_norm"], p["sink"])


def _rope(x, cos_ref, sin_ref, width):
    lane = lax.broadcasted_iota(jnp.int32, x.shape, 1) & (HEAD_DIM - 1)
    half = HEAD_DIM // 2
    swapped = jnp.where(lane < half, pltpu.roll(x, width - half, 1), pltpu.roll(x, half, 1))
    return x * cos_ref[...] + swapped * sin_ref[...]


def _lat_att_body(L, P, u_ref, kc_ref, vc_ref, qw_ref, kw_ref, sink_ref, cq_ref, sq_ref, ck_ref, sk_ref, y_ref):
    hd = HEAD_DIM
    scale = hd ** -0.5
    nb = L // ABLOCK
    q = _rope(_qk_norm(u_ref[:, 0:MIX], qw_ref[...], MIX), cq_ref, sq_ref, MIX)
    k = _rope(_qk_norm(u_ref[:, MIX:MIX + 2 * hd], kw_ref[...], 2 * hd), ck_ref, sk_ref, 2 * hd)
    v = u_ref[:, MIX + 2 * hd:MIX + 4 * hd]
    kc = kc_ref[0]
    vc = vc_ref[0]
    for i in range(nb):
        lo = max(i - 1, 0) * ABLOCK
        hi = min(i + 2, nb) * ABLOCK
        qpos = i * ABLOCK + lax.broadcasted_iota(jnp.int32, (ABLOCK, hi - lo), 0)
        kpos = lo + lax.broadcasted_iota(jnp.int32, (ABLOCK, hi - lo), 1)
        dist = qpos - kpos
        ok = (dist <= WINDOW) & (dist >= -WINDOW)
        outs = []
        for g in range(2):
            k_l = k[lo:hi, g * hd:(g + 1) * hd]
            v_l = v[lo:hi, g * hd:(g + 1) * hd]
            k_c = kc[:, g * hd:(g + 1) * hd]
            v_c = vc[:, g * hd:(g + 1) * hd]
            for r in range(2):
                h = 2 * g + r
                q_h = q[i * ABLOCK:(i + 1) * ABLOCK, h * hd:(h + 1) * hd]
                q_b = q_h.astype(bf16)
                s_loc = jnp.where(ok, _dot_nt(q_b, k_l.astype(bf16)) * scale, NEG_INF)
                s_ctx = _dot_nt(q_b, k_c.astype(bf16)) * scale
                outs.append(_sink_softmax_pv([s_loc, s_ctx], [v_l, v_c], sink_ref[0:1, h:h + 1]))
        y_ref[i * ABLOCK:(i + 1) * ABLOCK, :] = jnp.concatenate(outs, axis=1).astype(y_ref.dtype)


def latent_attention(u_att, t0, nseq, L, p, kc, vc, rope):
    b0 = t0 // L
    P = kc.shape[1]
    small = lambda s: pl.BlockSpec(s, lambda b: (0,) * len(s))
    return pl.pallas_call(
        functools.partial(_lat_att_body, L, P),
        grid=(nseq,),
        in_specs=[pl.BlockSpec((L, U_ATT), lambda b: (b0 + b, 0)),
                  pl.BlockSpec((1, P, 2 * HEAD_DIM), lambda b: (b, 0, 0)),
                  pl.BlockSpec((1, P, 2 * HEAD_DIM), lambda b: (b, 0, 0)),
                  small((1, MIX)), small((1, 2 * HEAD_DIM)), small((1, N_HEADS)),
                  small((L, MIX)), small((L, MIX)), small((L, 2 * HEAD_DIM)), small((L, 2 * HEAD_DIM))],
        out_specs=pl.BlockSpec((L, MIX), lambda b: (b, 0)),
        out_shape=jax.ShapeDtypeStruct((nseq * L, MIX), bf16),
        compiler_params=_params("arbitrary"),
        name="lat_att",
    )(u_att, kc, vc, p["q_norm"], p["k_norm"], p["sink"], *rope)


def rope_tables(L):
    rows = L // GRID_W
    row = jnp.repeat(jnp.arange(rows, dtype=f32), GRID_W)
    col = jnp.tile(jnp.arange(GRID_W, dtype=f32), rows)
    nf = HEAD_DIM // 4
    inv = ROPE_THETA ** (-jnp.arange(nf, dtype=f32) / nf)
    ang = jnp.concatenate([row[:, None] * inv, col[:, None] * inv], axis=-1)
    cos = jnp.concatenate([jnp.cos(ang), jnp.cos(ang)], axis=-1)
    sin = jnp.concatenate([-jnp.sin(ang), jnp.sin(ang)], axis=-1)
    return (jnp.tile(cos, (1, N_HEADS)), jnp.tile(sin, (1, N_HEADS)), jnp.tile(cos, (1, 2)), jnp.tile(sin, (1, 2)))


def _out_body(ya, yb, yc, yd, x_ref, w_ref, g1_ref, nw_ref, sc_ref, sh_ref, x1_ref, h2_ref, h2t_ref):
    y = _dot(ya[...], w_ref[0:MIX, :])
    y = y + _dot(yb[...], w_ref[MIX:2 * MIX, :])
    y = y + _dot(yc[...], w_ref[2 * MIX:3 * MIX, :])
    y = y + _dot(yd[...], w_ref[3 * MIX:4 * MIX, :])
    x1 = x_ref[...] + g1_ref[0] * y
    x1_ref[...] = x1
    h = x1 * lax.rsqrt(jnp.mean(x1 * x1, axis=-1, keepdims=True) + EPS) * nw_ref[...]
    h = h * (1.0 + sc_ref[0]) + sh_ref[0]
    h2_ref[...] = h.astype(bf16)
    h2t_ref[...] = h.T.astype(bf16)


def out_proj(ys, x, w_out, g1, nw, sc, sh, t_p, l_s):
    t = x.shape[0]
    tm = 512
    row = _mod_row_map(tm, t_p, l_s)
    tok = lambda w: pl.BlockSpec((tm, w), lambda i: (i, 0))
    return pl.pallas_call(
        _out_body,
        grid=(t // tm,),
        in_specs=[tok(MIX), tok(MIX), tok(MIX), tok(MIX), tok(D_MODEL),
                  pl.BlockSpec((D_MODEL, D_MODEL), lambda i: (0, 0)),
                  pl.BlockSpec((1, 1, D_MODEL), row),
                  pl.BlockSpec((1, D_MODEL), lambda i: (0, 0)),
                  pl.BlockSpec((1, 1, D_MODEL), row),
                  pl.BlockSpec((1, 1, D_MODEL), row)],
        out_specs=[tok(D_MODEL), tok(D_MODEL), pl.BlockSpec((D_MODEL, tm), lambda i: (0, i))],
        out_shape=[jax.ShapeDtypeStruct((t, D_MODEL), f32),
                   jax.ShapeDtypeStruct((t, D_MODEL), bf16),
                   jax.ShapeDtypeStruct((D_MODEL, t), bf16)],
        compiler_params=_params("arbitrary"),
        name="out_proj",
    )(*ys, x, w_out, g1, nw, sc, sh)


def _top16(s, n):
    io = lax.broadcasted_iota(jnp.int32, s.shape, 0).astype(f32)
    rank = jnp.full(s.shape, float(PEER_TOPK), f32)
    vals = []
    for r in range(PEER_TOPK):
        m = jnp.max(s, axis=0, keepdims=True)
        idx = jnp.min(jnp.where(s == m, io, float(n)), axis=0, keepdims=True)
        hit = io == idx
        rank = jnp.where(hit, float(r), rank)
        s = jnp.where(hit, -jnp.inf, s)
        vals.append(m)
    return jnp.concatenate(vals, axis=0), rank


def _peer_route_body(h2_ref, wq_ref, keys_ref, r2_ref, e2_ref, lim_ref, cw_ref):
    k = PEER_TOPK
    q = _dot(h2_ref[...], wq_ref[...])
    tn = q.shape[0]
    n_b = [k // (a + 1) for a in range(k)]
    n_cand = sum(n_b)
    n_rows = -(-n_cand // 8) * 8
    ra = lax.broadcasted_iota(jnp.int32, (k, n_rows), 0)
    rc = lax.broadcasted_iota(jnp.int32, (k, n_rows), 1)
    rep_a = jnp.zeros((k, n_rows), f32)
    start = 0
    for a in range(k):
        rep_a = jnp.where((ra == a) & (rc >= start) & (rc < start + n_b[a]), 1.0, rep_a)
        start += n_b[a]
    lw = 128
    pad = jnp.full((n_rows - n_cand, lw), -jnp.inf, f32)
    for h in range(PEER_HEADS):
        s1_all = _dot_nt(keys_ref[h, 0], q[:, h * 256:h * 256 + 128], HI)
        s2_all = _dot_nt(keys_ref[h, 1], q[:, h * 256 + 128:h * 256 + 256], HI)
        for j in range(tn // lw):
            cols = slice(j * lw, (j + 1) * lw)
            s1 = s1_all[:, cols]
            s2 = s2_all[:, cols]
            t1, rank1 = _top16(s1, PEER_NKEYS)
            t2, rank2 = _top16(s2, PEER_NKEYS)
            cand = jnp.concatenate([t1[a:a + 1, :] + t2[0:n_b[a], :] for a in range(k)] + [pad], axis=0)
            best, crank = _top16(cand, n_rows)
            sel = jnp.where(crank < k, 1.0, 0.0).astype(f32)
            count_a = _dot(rep_a, sel)
            zsum = jnp.sum(jnp.exp(best - best[0:1, :]), axis=0, keepdims=True)
            lim = jnp.zeros((PEER_NKEYS, lw), f32)
            for a in range(k):
                lim = jnp.where(rank1 == a, count_a[a:a + 1, :], lim)
            r2_ref[h, :, cols] = rank2.astype(bf16)
            e2_ref[h, :, cols] = jnp.exp(s2 - t2[0:1, :]).astype(bf16)
            lim_ref[h, :, cols] = lim
            cw_ref[h, :, cols] = jnp.exp(s1 - t1[0:1, :]) / zsum


def peer_route(h2, w_q, keys):
    t = h2.shape[0]
    tn = 256
    arr = jax.ShapeDtypeStruct((PEER_HEADS, PEER_NKEYS, t), f32)
    arr_h = jax.ShapeDtypeStruct((PEER_HEADS, PEER_NKEYS, t), bf16)
    spec = pl.BlockSpec((PEER_HEADS, PEER_NKEYS, tn), lambda i: (0, 0, i))
    return pl.pallas_call(
        _peer_route_body,
        grid=(t // tn,),
        in_specs=[pl.BlockSpec((tn, D_MODEL), lambda i: (i, 0)),
                  pl.BlockSpec((D_MODEL, 2 * D_MODEL), lambda i: (0, 0)),
                  pl.BlockSpec((PEER_HEADS, 2, PEER_NKEYS, PEER_NKEYS), lambda i: (0, 0, 0, 0))],
        out_specs=[spec, spec, spec, spec],
        out_shape=[arr_h, arr_h, arr, arr],
        compiler_params=_params("arbitrary"),
        name="peer_route",
    )(h2, w_q, keys)


def _peer_expert_body(eb_i, h2t_ref, u_ref, v_ref, r2_ref, e2_ref, lim_ref, cw_ref, x1_ref, g2_ref,
                      o_ref, wt, acc):
    e = pl.program_id(1)

    @pl.when(e == 0)
    def _():
        acc[...] = jnp.zeros_like(acc)

    sub_i = 4
    sub = sub_i * PEER_NKEYS
    n_sub = eb_i // sub_i

    def scores(c):
        return _dot(u_ref[pl.ds(c * sub, sub), :], h2t_ref[...])

    st_next = scores(0)
    for c in range(n_sub):
        rows = pl.ds(c * sub, sub)
        st = st_next
        if c + 1 < n_sub:
            st_next = scores(c + 1)
        for ii in range(sub_i):
            i = e * eb_i + c * sub_i + ii
            g = None
            for h in range(PEER_HEADS):
                lim = lim_ref[h, pl.ds(i, 1), :].astype(bf16)
                cw = cw_ref[h, pl.ds(i, 1), :].astype(bf16)
                term = jnp.where(r2_ref[h] < lim, e2_ref[h] * cw, jnp.zeros((), bf16))
                g = term if g is None else g + term
            s = st[ii * PEER_NKEYS:(ii + 1) * PEER_NKEYS, :]
            act = 0.5 * s * (1.0 + lax.erf(s * (2.0 ** -0.5)))
            wt[pl.ds(c * sub + ii * PEER_NKEYS, PEER_NKEYS), :] = g * act.astype(bf16)
        acc[...] += _dot_tn(wt[rows, :], v_ref[rows, :])

    @pl.when(e == pl.num_programs(1) - 1)
    def _():
        o_ref[...] = x1_ref[...] + g2_ref[0] * acc[...]


def peer_experts(h2t, u_tab, v_tab, route, x1, g2, t_p, l_s):
    t = x1.shape[0]
    tn = 512
    eb_i = 16
    eb = eb_i * PEER_NKEYS
    row = _mod_row_map(tn, t_p, l_s)
    rspec = pl.BlockSpec((PEER_HEADS, PEER_NKEYS, tn), lambda j, e: (0, 0, j))
    return pl.pallas_call(
        functools.partial(_peer_expert_body, eb_i),
        grid=(t // tn, PEER_EXPERTS // eb),
        in_specs=[pl.BlockSpec((D_MODEL, tn), lambda j, e: (0, j)),
                  pl.BlockSpec((eb, D_MODEL), lambda j, e: (e, 0)),
                  pl.BlockSpec((eb, D_MODEL), lambda j, e: (e, 0)),
                  rspec, rspec, rspec, rspec,
                  pl.BlockSpec((tn, D_MODEL), lambda j, e: (j, 0)),
                  pl.BlockSpec((1, 1, D_MODEL), lambda j, e: row(j))],
        out_specs=pl.BlockSpec((tn, D_MODEL), lambda j, e: (j, 0)),
        out_shape=jax.ShapeDtypeStruct((t, D_MODEL), f32),
        scratch_shapes=[pltpu.VMEM((eb, tn), bf16), pltpu.VMEM((tn, D_MODEL), f32)],
        compiler_params=_params("arbitrary", "arbitrary"),
        name="peer_experts",
    )(h2t, u_tab, v_tab, *route, x1, g2)


def _regroup_w_in(w):
    o_gdn = 776
    o_fn = o_gdn + 1040
    o_att = o_fn + 256
    main = jnp.concatenate([w[:, 0:768], w[:, o_gdn:o_gdn + 1024], w[:, o_fn:o_fn + 256], w[:, o_att:o_att + 512]], axis=1)
    small = jnp.concatenate([w[:, 768:776], w[:, o_gdn + 1024:o_gdn + 1040],
                             jnp.zeros((w.shape[0], U_SM - 24), w.dtype)], axis=1)
    return main.astype(bf16), small, small.T


def _layer_params(i, ssd_conv_w, ssd_conv_b, ssd_A_log, ssd_dt_bias, ssd_D, ssd_norm_w,
                  gdn_conv_w, gdn_conv_b, gdn_A_log, gdn_dt_bias, gdn_norm_w, q_norm_w, k_norm_w, att_sink):
    ssd = dict(conv_w=ssd_conv_w[i], conv_b=ssd_conv_b[i][None, :],
               al_row=ssd_A_log[i].reshape(1, 8), al_col=ssd_A_log[i].reshape(8, 1),
               db_row=ssd_dt_bias[i].reshape(1, 8), db_col=ssd_dt_bias[i].reshape(8, 1),
               d_skip=jnp.repeat(ssd_D[i], HEAD_DIM)[None, :], norm_w=ssd_norm_w[i][None, :])
    gdn = dict(conv_w=gdn_conv_w[i], conv_b=gdn_conv_b[i][None, :],
               al_row=gdn_A_log[i].reshape(1, 8), al_col=gdn_A_log[i].reshape(8, 1),
               db_row=gdn_dt_bias[i].reshape(1, 8), db_col=gdn_dt_bias[i].reshape(8, 1),
               norm_w=jnp.tile(gdn_norm_w[i], N_HEADS)[None, :])
    att = dict(q_norm=jnp.tile(q_norm_w[i], N_HEADS)[None, :], k_norm=jnp.tile(k_norm_w[i], 2)[None, :],
               sink=att_sink[i][None, :])
    return ssd, gdn, att


def kernel(x_prompt, x_sample, cache_k, cache_v, state_ssd, state_gdn, c, c_ctx, w_mod, b_mod, norm1_w, norm2_w, w_in, w_out, ssd_conv_w, ssd_conv_b, ssd_A_log, ssd_dt_bias, ssd_D, ssd_norm_w, gdn_conv_w, gdn_conv_b, gdn_A_log, gdn_dt_bias, gdn_norm_w, q_norm_w, k_norm_w, att_sink, peer_w_q, peer_keys, peer_u, peer_v):
    nb_p, l_p, d = x_prompt.shape
    nb_s, l_s, _ = x_sample.shape
    past = cache_k.shape[2]
    t_p = nb_p * l_p
    x = jnp.concatenate([x_prompt.reshape(t_p, d), x_sample.reshape(nb_s * l_s, d)], axis=0)

    cvecs = jnp.concatenate([c_ctx[None, :], c, jnp.zeros((8 - 1 - nb_s, d), f32)], axis=0)
    mods = adaln_all(cvecs, w_mod, b_mod)
    dft_p, dft_s = dft_tables(l_p), dft_tables(l_s)
    rope = rope_tables(l_s)
    zeros_state = jnp.zeros((nb_p, 2, N_HEADS, HEAD_DIM, HEAD_DIM), f32)

    ks, vs, ssd_states, gdn_states = [], [], [], []
    for i in range(DEPTH):
        sh1, sc1, g1, sh2, sc2, g2 = [mods[i, :, j * d:(j + 1) * d].reshape(8, 1, d) for j in range(6)]
        ssd_p, gdn_p, att_p = _layer_params(i, ssd_conv_w, ssd_conv_b, ssd_A_log, ssd_dt_bias, ssd_D, ssd_norm_w,
                                            gdn_conv_w, gdn_conv_b, gdn_A_log, gdn_dt_bias, gdn_norm_w,
                                            q_norm_w, k_norm_w, att_sink)
        w_main, w_sm, w_smt = _regroup_w_in(w_in[i])
        u_ssd, u_gdn, u_fn, u_att, u_sm, u_smt = in_proj(x, norm1_w[i][None, :], sc1, sh1, w_main, w_sm, w_smt, t_p, l_s)

        y_ssd_p, st_ssd = ssd_mixer(u_ssd, u_sm, u_smt, 0, nb_p, l_p, ssd_p, zeros_state)
        y_ssd_s, _ = ssd_mixer(u_ssd, u_sm, u_smt, t_p, nb_s, l_s, ssd_p, state_ssd[:, i])
        y_gdn_p, st_gdn = gdn_mixer(u_gdn, u_sm, u_smt, 0, nb_p, l_p, gdn_p, zeros_state)
        y_gdn_s, _ = gdn_mixer(u_gdn, u_sm, u_smt, t_p, nb_s, l_s, gdn_p, state_gdn[:, i])
        y_fn_p = fnet_mixer(u_fn, 0, nb_p, l_p, dft_p)
        y_fn_s = fnet_mixer(u_fn, t_p, nb_s, l_s, dft_s)
        y_att_p, k_new, v_new = ctx_attention(u_att, 0, nb_p, l_p, att_p)
        y_att_s = latent_attention(u_att, t_p, nb_s, l_s, att_p,
                                   cache_k[:, i].reshape(nb_s, past, 2 * HEAD_DIM),
                                   cache_v[:, i].reshape(nb_s, past, 2 * HEAD_DIM), rope)
        ys = [jnp.concatenate([a, b], axis=0) for a, b in
              ((y_ssd_p, y_ssd_s), (y_gdn_p, y_gdn_s), (y_fn_p, y_fn_s), (y_att_p, y_att_s))]

        x1, h2, h2t = out_proj(ys, x, w_out[i].astype(bf16), g1, norm2_w[i][None, :], sc2, sh2, t_p, l_s)
        route = peer_route(h2, peer_w_q[i].astype(bf16), peer_keys[i])
        x = peer_experts(h2t, peer_u[i].astype(bf16), peer_v[i].astype(bf16), route, x1, g2, t_p, l_s)

        ks.append(k_new.reshape(nb_p, l_p, 2, HEAD_DIM))
        vs.append(v_new.reshape(nb_p, l_p, 2, HEAD_DIM))
        ssd_states.append(st_ssd)
        gdn_states.append(st_gdn)

    return (x[:t_p].reshape(nb_p, l_p, d), x[t_p:].reshape(nb_s, l_s, d),
            jnp.stack(ks, axis=1), jnp.stack(vs, axis=1),
            jnp.stack(ssd_states, axis=1), jnp.stack(gdn_states, axis=1))
```

```python
import functools
import math

import jax
import jax.numpy as jnp
from jax import lax
from jax.experimental import pallas as pl
from jax.experimental.pallas import tpu as pltpu

f32 = jnp.float32
bf16 = jnp.bfloat16
HI = lax.Precision.HIGHEST

D_MODEL = 1024
DEPTH = 4
EPS = 1e-6
NEG_INF = -1e30
CONV_W = 5
HEAD_DIM = 64
N_HEADS = 4
MIX = 256
SSD_CHUNK = 256
GDN_CHUNK = 64
WINDOW = 128
ABLOCK = 128
GRID_W = 64
ROPE_THETA = 10000.0
PEER_HEADS = 8
PEER_NKEYS = 128
PEER_TOPK = 16
PEER_EXPERTS = PEER_NKEYS * PEER_NKEYS

VMEM_LIMIT = 56 * 1024 * 1024

U_SSD, U_GDN, U_FN, U_ATT, U_SM = 768, 1024, 256, 512, 128
U_MAIN = U_SSD + U_GDN + U_FN + U_ATT


def _dot(a, b, prec=None):
    return jnp.dot(a, b, preferred_element_type=f32, precision=prec)


def _dot_nt(a, b, prec=None):
    return lax.dot_general(a, b, (((1,), (1,)), ((), ())), preferred_element_type=f32, precision=prec)


def _dot_tn(a, b, prec=None):
    return lax.dot_general(a, b, (((0,), (0,)), ((), ())), preferred_element_type=f32, precision=prec)


def _split_bf16(x):
    hi = x.astype(bf16)
    return hi, (x - hi.astype(f32)).astype(bf16)


def _dot3(a, b):
    ah, al = _split_bf16(a)
    bh, bl = _split_bf16(b)
    return _dot(ah, bh) + (_dot(ah, bl) + _dot(al, bh))


def _dotb(a, b):
    return _dot(a.astype(bf16), b.astype(bf16))


def _sigmoid(x):
    return 1.0 / (1.0 + jnp.exp(-x))


def _silu(x):
    return x * _sigmoid(x)


def _softplus(x):
    return jnp.maximum(x, 0.0) + jnp.log(1.0 + jnp.exp(-jnp.abs(x)))


def _params(*sem):
    return pltpu.CompilerParams(dimension_semantics=sem, vmem_limit_bytes=VMEM_LIMIT)


def _idiv(x, n):
    return lax.shift_right_logical(x, int(math.log2(n)))


def _head_mean_matrix(width):
    r = _idiv(lax.broadcasted_iota(jnp.int32, (width, width), 0), HEAD_DIM)
    c = _idiv(lax.broadcasted_iota(jnp.int32, (width, width), 1), HEAD_DIM)
    return jnp.where(r == c, 1.0 / HEAD_DIM, 0.0).astype(f32)


def _shift_rows(x, d, n):
    row = lax.broadcasted_iota(jnp.int32, x.shape, 0)
    y = pltpu.roll(x, (-d) % n, 0)
    ok = (row + d >= 0) & (row + d < n)
    return jnp.where(ok, y, 0.0)


def _conv_silu(x, w_ref, b_ref, n):
    half = (CONV_W - 1) // 2
    acc = b_ref[...] + w_ref[half:half + 1, :] * x
    for k in range(CONV_W):
        if k != half:
            acc = acc + w_ref[k:k + 1, :] * _shift_rows(x, k - half, n)
    return _silu(acc)


def _mod_body(c_ref, w_ref, b_ref, o_ref):
    o_ref[0] = _dot(_silu(c_ref[...]), w_ref[0], HI) + b_ref[0]


def adaln_all(cvecs, w_mod, b_mod):
    tn = 1536
    n = w_mod.shape[-1]
    return pl.pallas_call(
        _mod_body,
        grid=(DEPTH, n // tn),
        in_specs=[pl.BlockSpec((8, D_MODEL), lambda l, j: (0, 0)),
                  pl.BlockSpec((1, D_MODEL, tn), lambda l, j: (l, 0, j)),
                  pl.BlockSpec((1, 1, tn), lambda l, j: (l, 0, j))],
        out_specs=pl.BlockSpec((1, 8, tn), lambda l, j: (l, 0, j)),
        out_shape=jax.ShapeDtypeStruct((DEPTH, 8, n), f32),
        compiler_params=_params("arbitrary", "arbitrary"),
        name="adaln",
    )(cvecs, w_mod, b_mod.reshape(DEPTH, 1, n))


def _in_body(x_ref, nw_ref, sc_ref, sh_ref, w_ref, ws_ref, wst_ref,
             o_ssd, o_gdn, o_fn, o_att, o_sm, o_smt):
    x = x_ref[...]
    h = x * lax.rsqrt(jnp.mean(x * x, axis=-1, keepdims=True) + EPS) * nw_ref[...]
    h = h * (1.0 + sc_ref[0]) + sh_ref[0]
    u = _dot(h.astype(bf16), w_ref[...])
    o_ssd[...] = u[:, 0:U_SSD]
    o_gdn[...] = u[:, U_SSD:U_SSD + U_GDN]
    o_fn[...] = u[:, U_SSD + U_GDN:U_SSD + U_GDN + U_FN]
    o_att[...] = u[:, U_SSD + U_GDN + U_FN:U_MAIN]
    o_sm[...] = _dot(h, ws_ref[...], HI)
    o_smt[...] = _dot_nt(wst_ref[...], h, HI)


def _mod_row_map(tm, t_p, l_s):
    def f(i):
        t0 = i * tm
        return (jnp.where(t0 < t_p, 0, 1 + (t0 - t_p) // l_s), 0, 0)
    return f


def in_proj(x, nw, sc, sh, w_main, w_sm, w_smt, t_p, l_s):
    t = x.shape[0]
    tm = 512
    row = _mod_row_map(tm, t_p, l_s)
    widths = (U_SSD, U_GDN, U_FN, U_ATT, U_SM)
    return pl.pallas_call(
        _in_body,
        grid=(t // tm,),
        in_specs=[pl.BlockSpec((tm, D_MODEL), lambda i: (i, 0)),
                  pl.BlockSpec((1, D_MODEL), lambda i: (0, 0)),
                  pl.BlockSpec((1, 1, D_MODEL), row),
                  pl.BlockSpec((1, 1, D_MODEL), row),
                  pl.BlockSpec((D_MODEL, U_MAIN), lambda i: (0, 0)),
                  pl.BlockSpec((D_MODEL, U_SM), lambda i: (0, 0)),
                  pl.BlockSpec((U_SM, D_MODEL), lambda i: (0, 0))],
        out_specs=[pl.BlockSpec((tm, w), lambda i: (i, 0)) for w in widths]
        + [pl.BlockSpec((U_SM, tm), lambda i: (0, i))],
        out_shape=[jax.ShapeDtypeStruct((t, w), f32) for w in widths]
        + [jax.ShapeDtypeStruct((U_SM, t), f32)],
        compiler_params=_params("arbitrary"),
        name="in_proj",
    )(x, nw, sc, sh, w_main, w_sm, w_smt)


def _ssd_body(L, u_ref, sm_ref, smt_ref, cw_ref, cb_ref, alr_ref, alc_ref, dbr_ref, dbc_ref,
              dsk_ref, nw_ref, h0_ref, y_ref, hT_ref, yacc):
    Q = min(L, SSD_CHUNK)
    nc = L // Q
    hd = HEAD_DIM
    z = u_ref[:, 0:MIX]
    xbc = _conv_silu(u_ref[:, MIX:U_SSD], cw_ref, cb_ref, L)
    xs = xbc[:, 0:MIX]
    dtc = _softplus(sm_ref[:, 0:8] + dbr_ref[...])
    dtr = _softplus(smt_ref[0:8, :] + dbc_ref[...])
    ac = dtc * (-jnp.exp(alr_ref[...]))
    ar = dtr * (-jnp.exp(alc_ref[...]))
    ri = lax.broadcasted_iota(jnp.int32, (Q, Q), 0)
    ci = lax.broadcasted_iota(jnp.int32, (Q, Q), 1)
    low = ri >= ci
    upp = ci >= ri
    tril = jnp.where(low, 1.0, 0.0).astype(f32)

    def chunk_terms(c):
        r0 = c * Q
        a_c = ac[r0:r0 + Q, :]
        a_r = ar[:, r0:r0 + Q]
        pc = _dot(tril, a_c, HI)
        pr = _dot_nt(a_r, tril, HI)
        return a_c, a_r, pc, pr

    hf = [h0_ref[0, 0, h] for h in range(N_HEADS)]
    for c in range(nc):
        r0 = c * Q
        a_c, a_r, pc, pr = chunk_terms(c)
        ys = []
        for h in range(N_HEADS):
            g = h // 2
            x_h = xs[r0:r0 + Q, h * hd:(h + 1) * hd]
            b_g = xbc[r0:r0 + Q, MIX + g * hd:MIX + (g + 1) * hd]
            c_g = xbc[r0:r0 + Q, MIX + 2 * hd + g * hd:MIX + 2 * hd + (g + 1) * hd]
            gm = _dot_nt(c_g, b_g, HI)
            hb = N_HEADS + h
            seg_f = pc[:, h:h + 1] - pr[h:h + 1, :]
            l_f = jnp.where(low, jnp.exp(jnp.minimum(seg_f, 0.0)), 0.0)
            e_c = pc[:, hb:hb + 1] - a_c[:, hb:hb + 1]
            e_r = pr[hb:hb + 1, :] - a_r[hb:hb + 1, :]
            seg_b = e_r - e_c
            l_b = jnp.where(upp, jnp.exp(jnp.minimum(seg_b, 0.0)), 0.0)
            m = gm * (l_f * dtr[h:h + 1, r0:r0 + Q] + l_b * dtr[hb:hb + 1, r0:r0 + Q])
            y_h = _dot(m, x_h, HI)
            y_h = y_h + _dot_nt(c_g * jnp.exp(pc[:, h:h + 1]), hf[h], HI)
            tot = pc[Q - 1:Q, h:h + 1]
            wgt = dtc[r0:r0 + Q, h:h + 1] * jnp.exp(tot - pc[:, h:h + 1])
            hf[h] = hf[h] * jnp.exp(tot) + _dot_tn(x_h * wgt, b_g, HI)
            ys.append(y_h)
        yacc[r0:r0 + Q, :] = jnp.concatenate(ys, axis=1)
    hb_s = [h0_ref[0, 1, h] for h in range(N_HEADS)]
    for c in range(nc - 1, -1, -1):
        r0 = c * Q
        a_c, a_r, pc, pr = chunk_terms(c)
        ys = []
        for h in range(N_HEADS):
            g = h // 2
            hb = N_HEADS + h
            x_h = xs[r0:r0 + Q, h * hd:(h + 1) * hd]
            b_g = xbc[r0:r0 + Q, MIX + g * hd:MIX + (g + 1) * hd]
            c_g = xbc[r0:r0 + Q, MIX + 2 * hd + g * hd:MIX + 2 * hd + (g + 1) * hd]
            e_c = pc[:, hb:hb + 1] - a_c[:, hb:hb + 1]
            tot = pc[Q - 1:Q, hb:hb + 1]
            ys.append(_dot_nt(c_g * jnp.exp(tot - e_c), hb_s[h], HI))
            wgt = dtc[r0:r0 + Q, hb:hb + 1] * jnp.exp(e_c)
            hb_s[h] = hb_s[h] * jnp.exp(tot) + _dot_tn(x_h * wgt, b_g, HI)
        yacc[r0:r0 + Q, :] = yacc[r0:r0 + Q, :] + jnp.concatenate(ys, axis=1)
    for h in range(N_HEADS):
        hT_ref[0, 0, h] = hf[h]
        hT_ref[0, 1, h] = hb_s[h]
    y = yacc[...] + dsk_ref[...] * xs
    y = y * _silu(z)
    ms = _dot(y * y, _head_mean_matrix(MIX), HI)
    y_ref[...] = (y * lax.rsqrt(ms + EPS) * nw_ref[...]).astype(y_ref.dtype)


def ssd_mixer(u_ssd, u_sm, u_smt, t0, nseq, L, p, h0):
    b0 = t0 // L
    small = lambda s: pl.BlockSpec(s, lambda b: (0,) * len(s))
    return pl.pallas_call(
        functools.partial(_ssd_body, L),
        grid=(nseq,),
        in_specs=[pl.BlockSpec((L, U_SSD), lambda b: (b0 + b, 0)),
                  pl.BlockSpec((L, U_SM), lambda b: (b0 + b, 0)),
                  pl.BlockSpec((U_SM, L), lambda b: (0, b0 + b)),
                  small((CONV_W, 512)), small((1, 512)),
                  small((1, 8)), small((8, 1)), small((1, 8)), small((8, 1)),
                  small((1, MIX)), small((1, MIX)),
                  pl.BlockSpec((1, 2, N_HEADS, HEAD_DIM, HEAD_DIM), lambda b: (b, 0, 0, 0, 0))],
        out_specs=[pl.BlockSpec((L, MIX), lambda b: (b, 0)),
                   pl.BlockSpec((1, 2, N_HEADS, HEAD_DIM, HEAD_DIM), lambda b: (b, 0, 0, 0, 0))],
        out_shape=[jax.ShapeDtypeStruct((nseq * L, MIX), bf16),
                   jax.ShapeDtypeStruct((nseq, 2, N_HEADS, HEAD_DIM, HEAD_DIM), f32)],
        scratch_shapes=[pltpu.VMEM((L, MIX), f32)],
        compiler_params=_params("arbitrary"),
        name=f"ssd_{L}",
    )(u_ssd, u_sm, u_smt, p["conv_w"], p["conv_b"], p["al_row"], p["al_col"], p["db_row"], p["db_col"],
      p["d_skip"], p["norm_w"], h0)


def _gdn_body(L, u_ref, sm_ref, smt_ref, cw_ref, cb_ref, alr_ref, alc_ref, dbr_ref, dbc_ref,
              nw_ref, s0_ref, y_ref, sT_ref, qkv, oacc):
    Q = GDN_CHUNK
    nc = L // Q
    W = MIX
    qkv[...] = _conv_silu(u_ref[:, 0:3 * MIX], cw_ref, cb_ref, L)
    hm = _head_mean_matrix(MIX) * float(HEAD_DIM)
    q_all = qkv[:, 0:MIX]
    k_all = qkv[:, MIX:2 * MIX]
    qkv[:, 0:MIX] = q_all * lax.rsqrt(_dot(q_all * q_all, hm, HI) + EPS) * (HEAD_DIM ** -0.5)
    qkv[:, MIX:2 * MIX] = k_all * lax.rsqrt(_dot(k_all * k_all, hm, HI) + EPS)
    lac = -jnp.exp(alr_ref[...]) * _softplus(sm_ref[:, 8:16] + dbr_ref[...])
    lar = -jnp.exp(alc_ref[...]) * _softplus(smt_ref[8:16, :] + dbc_ref[...])
    btc = _sigmoid(sm_ref[:, 16:24])
    btr = _sigmoid(smt_ref[16:24, :])

    ri = lax.broadcasted_iota(jnp.int32, (W, W), 0)
    ci = lax.broadcasted_iota(jnp.int32, (W, W), 1)
    same_head = _idiv(ri, Q) == _idiv(ci, Q)
    blk16 = _idiv(ri, 16) == _idiv(ci, 16)
    eye = jnp.where(ri == ci, 1.0, 0.0).astype(f32)
    rq = lax.broadcasted_iota(jnp.int32, (Q, Q), 0)
    cq = lax.broadcasted_iota(jnp.int32, (Q, Q), 1)
    tril_q = jnp.where(rq >= cq, 1.0, 0.0).astype(f32)
    triu_q = jnp.where(cq >= rq, 1.0, 0.0).astype(f32)

    def stack_cols(m, base):
        return jnp.concatenate([m[:, base + h:base + h + 1] for h in range(N_HEADS)], axis=0)

    def stack_rows(m, base):
        return jnp.concatenate([m[base + h:base + h + 1, :] for h in range(N_HEADS)], axis=1)

    def block_diag(m):
        return jnp.where(same_head, jnp.concatenate([m] * N_HEADS, axis=1), 0.0)

    def heads_to_rows(m):
        return jnp.concatenate([m[:, h * HEAD_DIM:(h + 1) * HEAD_DIM] for h in range(N_HEADS)], axis=0)

    def rows_to_heads(m):
        return jnp.concatenate([m[h * Q:(h + 1) * Q, :] for h in range(N_HEADS)], axis=1)

    def sweep(d, S):
        tri_mat = tril_q if d == 0 else triu_q
        causal = (ri >= ci) if d == 0 else (ci >= ri)
        strict = (ri > ci) if d == 0 else (ci > ri)
        order = range(nc) if d == 0 else range(nc - 1, -1, -1)
        for c in order:
            r0 = c * Q
            qs = heads_to_rows(qkv[r0:r0 + Q, 0:MIX])
            ks = heads_to_rows(qkv[r0:r0 + Q, MIX:2 * MIX])
            vs = heads_to_rows(qkv[r0:r0 + Q, 2 * MIX:3 * MIX])
            la_c = lac[r0:r0 + Q, :]
            la_r = lar[:, r0:r0 + Q]
            gc = stack_cols(_dot(tri_mat, la_c, HI), 4 * d)
            gr = stack_rows(_dot_nt(la_r, tri_mat, HI), 4 * d)
            bc = stack_cols(btc[r0:r0 + Q, :], 4 * d)
            ends = []
            for h in range(N_HEADS):
                e = gr[:, h * Q + Q - 1:h * Q + Q] if d == 0 else gr[:, h * Q:h * Q + 1]
                ends.append(jnp.broadcast_to(e, (Q, 1)))
            g_end = jnp.concatenate(ends, axis=0)
            decay = jnp.where(causal & same_head, jnp.exp(jnp.minimum(gc - gr, 0.0)), 0.0)
            kbd = block_diag(ks).astype(bf16)
            kk = _dot_nt(kbd, kbd)
            a = jnp.where(strict, bc * kk * decay, 0.0)
            n = jnp.where(blk16, a, 0.0)
            n2 = _dotb(n, n)
            n4 = _dotb(n2, n2)
            n8 = _dotb(n4, n4)
            dinv = _dotb(_dotb(eye - n, eye + n2), _dotb(eye + n4, eye + n8))
            zz = _dotb(dinv, a - n)
            z2 = _dotb(zz, zz)
            tinv = _dotb(_dotb(eye - zz, eye + z2), dinv)
            rhs = jnp.concatenate([vs * bc, ks * (bc * jnp.exp(gc))], axis=1)
            sol = _dotb(tinv, rhs)
            u_c = sol[:, 0:HEAD_DIM]
            w_c = sol[:, HEAD_DIM:2 * HEAD_DIM]
            s_b = S.astype(bf16)
            attn = jnp.where(causal, _dot_nt(block_diag(qs).astype(bf16), kbd) * decay, 0.0)
            v_new = u_c - _dot(block_diag(w_c).astype(bf16), s_b)
            v_b = v_new.astype(bf16)
            o = _dot(block_diag(qs * jnp.exp(gc)).astype(bf16), s_b) + _dot(attn.astype(bf16), v_b)
            S = S * jnp.exp(g_end) + _dot_tn(block_diag(ks * jnp.exp(g_end - gc)).astype(bf16), v_b)
            o = rows_to_heads(o)
            if d == 0:
                oacc[r0:r0 + Q, :] = o
            else:
                oacc[r0:r0 + Q, :] = oacc[r0:r0 + Q, :] + o
        return S

    for d in range(2):
        S0 = jnp.concatenate([s0_ref[0, d, h] for h in range(N_HEADS)], axis=0)
        S = sweep(d, S0)
        for h in range(N_HEADS):
            sT_ref[0, d, h] = S[h * HEAD_DIM:(h + 1) * HEAD_DIM, :]
    o = oacc[...]
    ms = _dot(o * o, _head_mean_matrix(MIX), HI)
    o = o * lax.rsqrt(ms + EPS) * nw_ref[...]
    y_ref[...] = (o * _silu(u_ref[:, 3 * MIX:4 * MIX])).astype(y_ref.dtype)


def gdn_mixer(u_gdn, u_sm, u_smt, t0, nseq, L, p, s0):
    b0 = t0 // L
    small = lambda s: pl.BlockSpec(s, lambda b: (0,) * len(s))
    return pl.pallas_call(
        functools.partial(_gdn_body, L),
        grid=(nseq,),
        in_specs=[pl.BlockSpec((L, U_GDN), lambda b: (b0 + b, 0)),
                  pl.BlockSpec((L, U_SM), lambda b: (b0 + b, 0)),
                  pl.BlockSpec((U_SM, L), lambda b: (0, b0 + b)),
                  small((CONV_W, 3 * MIX)), small((1, 3 * MIX)),
                  small((1, 8)), small((8, 1)), small((1, 8)), small((8, 1)),
                  small((1, MIX)),
                  pl.BlockSpec((1, 2, N_HEADS, HEAD_DIM, HEAD_DIM), lambda b: (b, 0, 0, 0, 0))],
        out_specs=[pl.BlockSpec((L, MIX), lambda b: (b, 0)),
                   pl.BlockSpec((1, 2, N_HEADS, HEAD_DIM, HEAD_DIM), lambda b: (b, 0, 0, 0, 0))],
        out_shape=[jax.ShapeDtypeStruct((nseq * L, MIX), bf16),
                   jax.ShapeDtypeStruct((nseq, 2, N_HEADS, HEAD_DIM, HEAD_DIM), f32)],
        scratch_shapes=[pltpu.VMEM((L, 3 * MIX), f32), pltpu.VMEM((L, MIX), f32)],
        compiler_params=_params("arbitrary"),
        name=f"gdn_{L}",
    )(u_gdn, u_sm, u_smt, p["conv_w"], p["conv_b"], p["al_row"], p["al_col"], p["db_row"], p["db_col"],
      p["norm_w"], s0)


def _fnet_body(x_ref, cl_ref, sl_ref, cc_ref, sc_ref, y_ref):
    x = x_ref[...]
    xc = _dot(x, cc_ref[...], HI)
    xs = _dot(x, sc_ref[...], HI)
    y_ref[...] = (_dot(cl_ref[...], xc, HI) - _dot(sl_ref[...], xs, HI)).astype(y_ref.dtype)


def fnet_mixer(u_fn, t0, nseq, L, tabs):
    b0 = t0 // L
    full = lambda s: pl.BlockSpec(s, lambda b: (0,) * len(s))
    return pl.pallas_call(
        _fnet_body,
        grid=(nseq,),
        in_specs=[pl.BlockSpec((L, MIX), lambda b: (b0 + b, 0)),
                  full((L, L)), full((L, L)), full((MIX, MIX)), full((MIX, MIX))],
        out_specs=pl.BlockSpec((L, MIX), lambda b: (b, 0)),
        out_shape=jax.ShapeDtypeStruct((nseq * L, MIX), bf16),
        compiler_params=_params("arbitrary"),
        name=f"fnet_{L}",
    )(u_fn, *tabs)


def dft_tables(L):
    n = jnp.arange(L, dtype=jnp.int32)
    ang_l = (2.0 * math.pi / L) * ((n[:, None] * n[None, :]) % L).astype(f32)
    m = jnp.arange(MIX, dtype=jnp.int32)
    same = (m[:, None] // HEAD_DIM) == (m[None, :] // HEAD_DIM)
    ang_c = (2.0 * math.pi / HEAD_DIM) * (((m[:, None] % HEAD_DIM) * (m[None, :] % HEAD_DIM)) % HEAD_DIM).astype(f32)
    sl = 1.0 / math.sqrt(L)
    sc = 1.0 / math.sqrt(HEAD_DIM)
    return (jnp.cos(ang_l) * sl, jnp.sin(ang_l) * sl,
            jnp.where(same, jnp.cos(ang_c) * sc, 0.0), jnp.where(same, jnp.sin(ang_c) * sc, 0.0))


def _qk_norm(x, w_row, width):
    ms = _dot(x * x, _head_mean_matrix(width), HI)
    return x * lax.rsqrt(ms + EPS) * w_row


def _sink_softmax_pv(s_list, v_list, sink):
    m = sink
    for s in s_list:
        m = jnp.maximum(m, jnp.max(s, axis=-1, keepdims=True))
    den = jnp.exp(sink - m)
    acc = None
    for s, v in zip(s_list, v_list):
        e = jnp.exp(s - m)
        den = den + jnp.sum(e, axis=-1, keepdims=True)
        pv = _dotb(e, v)
        acc = pv if acc is None else acc + pv
    return acc / den


def _ctx_att_body(L, u_ref, qw_ref, kw_ref, sink_ref, y_ref, k_ref, v_ref):
    hd = HEAD_DIM
    scale = hd ** -0.5
    q = _qk_norm(u_ref[:, 0:MIX], qw_ref[...], MIX)
    k = _qk_norm(u_ref[:, MIX:MIX + 2 * hd], kw_ref[...], 2 * hd)
    v = u_ref[:, MIX + 2 * hd:MIX + 4 * hd]
    k_ref[...] = k
    v_ref[...] = v
    outs = []
    for g in range(2):
        k_g = k[:, g * hd:(g + 1) * hd]
        v_g = v[:, g * hd:(g + 1) * hd]
        for r in range(2):
            h = 2 * g + r
            s = _dot_nt(q[:, h * hd:(h + 1) * hd].astype(bf16), k_g.astype(bf16)) * scale
            outs.append(_sink_softmax_pv([s], [v_g], sink_ref[0:1, h:h + 1]))
    y_ref[...] = jnp.concatenate(outs, axis=1).astype(y_ref.dtype)


def ctx_attention(u_att, t0, nseq, L, p):
    b0 = t0 // L
    small = lambda s: pl.BlockSpec(s, lambda b: (0,) * len(s))
    return pl.pallas_call(
        functools.partial(_ctx_att_body, L),
        grid=(nseq,),
        in_specs=[pl.BlockSpec((L, U_ATT), lambda b: (b0 + b, 0)),
                  small((1, MIX)), small((1, 2 * HEAD_DIM)), small((1, N_HEADS))],
        out_specs=[pl.BlockSpec((L, MIX), lambda b: (b, 0)),
                   pl.BlockSpec((L, 2 * HEAD_DIM), lambda b: (b, 0)),
                   pl.BlockSpec((L, 2 * HEAD_DIM), lambda b: (b, 0))],
        out_shape=[jax.ShapeDtypeStruct((nseq * L, MIX), bf16),
                   jax.ShapeDtypeStruct((nseq * L, 2 * HEAD_DIM), f32),
                   jax.ShapeDtypeStruct((nseq * L, 2 * HEAD_DIM), f32)],
        compiler_params=_params("arbitrary"),
        name="ctx_att",
    )(u_att, p["q_norm"], p["k_norm"], p["sink"])


def _rope(x, cos_ref, sin_ref, width):
    lane = lax.broadcasted_iota(jnp.int32, x.shape, 1) & (HEAD_DIM - 1)
    half = HEAD_DIM // 2
    swapped = jnp.where(lane < half, pltpu.roll(x, width - half, 1), pltpu.roll(x, half, 1))
    return x * cos_ref[...] + swapped * sin_ref[...]


def _lat_att_body(L, P, u_ref, kc_ref, vc_ref, qw_ref, kw_ref, sink_ref, cq_ref, sq_ref, ck_ref, sk_ref, y_ref):
    hd = HEAD_DIM
    scale = hd ** -0.5
    nb = L // ABLOCK
    q = _rope(_qk_norm(u_ref[:, 0:MIX], qw_ref[...], MIX), cq_ref, sq_ref, MIX)
    k = _rope(_qk_norm(u_ref[:, MIX:MIX + 2 * hd], kw_ref[...], 2 * hd), ck_ref, sk_ref, 2 * hd)
    v = u_ref[:, MIX + 2 * hd:MIX + 4 * hd]
    kc = kc_ref[0]
    vc = vc_ref[0]
    for i in range(nb):
        lo = max(i - 1, 0) * ABLOCK
        hi = min(i + 2, nb) * ABLOCK
        qpos = i * ABLOCK + lax.broadcasted_iota(jnp.int32, (ABLOCK, hi - lo), 0)
        kpos = lo + lax.broadcasted_iota(jnp.int32, (ABLOCK, hi - lo), 1)
        dist = qpos - kpos
        ok = (dist <= WINDOW) & (dist >= -WINDOW)
        outs = []
        for g in range(2):
            k_l = k[lo:hi, g * hd:(g + 1) * hd]
            v_l = v[lo:hi, g * hd:(g + 1) * hd]
            k_c = kc[:, g * hd:(g + 1) * hd]
            v_c = vc[:, g * hd:(g + 1) * hd]
            for r in range(2):
                h = 2 * g + r
                q_h = q[i * ABLOCK:(i + 1) * ABLOCK, h * hd:(h + 1) * hd]
                q_b = q_h.astype(bf16)
                s_loc = jnp.where(ok, _dot_nt(q_b, k_l.astype(bf16)) * scale, NEG_INF)
                s_ctx = _dot_nt(q_b, k_c.astype(bf16)) * scale
                outs.append(_sink_softmax_pv([s_loc, s_ctx], [v_l, v_c], sink_ref[0:1, h:h + 1]))
        y_ref[i * ABLOCK:(i + 1) * ABLOCK, :] = jnp.concatenate(outs, axis=1).astype(y_ref.dtype)


def latent_attention(u_att, t0, nseq, L, p, kc, vc, rope):
    b0 = t0 // L
    P = kc.shape[1]
    small = lambda s: pl.BlockSpec(s, lambda b: (0,) * len(s))
    return pl.pallas_call(
        functools.partial(_lat_att_body, L, P),
        grid=(nseq,),
        in_specs=[pl.BlockSpec((L, U_ATT), lambda b: (b0 + b, 0)),
                  pl.BlockSpec((1, P, 2 * HEAD_DIM), lambda b: (b, 0, 0)),
                  pl.BlockSpec((1, P, 2 * HEAD_DIM), lambda b: (b, 0, 0)),
                  small((1, MIX)), small((1, 2 * HEAD_DIM)), small((1, N_HEADS)),
                  small((L, MIX)), small((L, MIX)), small((L, 2 * HEAD_DIM)), small((L, 2 * HEAD_DIM))],
        out_specs=pl.BlockSpec((L, MIX), lambda b: (b, 0)),
        out_shape=jax.ShapeDtypeStruct((nseq * L, MIX), bf16),
        compiler_params=_params("arbitrary"),
        name="lat_att",
    )(u_att, kc, vc, p["q_norm"], p["k_norm"], p["sink"], *rope)


def rope_tables(L):
    rows = L // GRID_W
    row = jnp.repeat(jnp.arange(rows, dtype=f32), GRID_W)
    col = jnp.tile(jnp.arange(GRID_W, dtype=f32), rows)
    nf = HEAD_DIM // 4
    inv = ROPE_THETA ** (-jnp.arange(nf, dtype=f32) / nf)
    ang = jnp.concatenate([row[:, None] * inv, col[:, None] * inv], axis=-1)
    cos = jnp.concatenate([jnp.cos(ang), jnp.cos(ang)], axis=-1)
    sin = jnp.concatenate([-jnp.sin(ang), jnp.sin(ang)], axis=-1)
    return (jnp.tile(cos, (1, N_HEADS)), jnp.tile(sin, (1, N_HEADS)), jnp.tile(cos, (1, 2)), jnp.tile(sin, (1, 2)))


def _out_body(ya, yb, yc, yd, x_ref, w_ref, g1_ref, nw_ref, sc_ref, sh_ref, x1_ref, h2_ref, h2t_ref):
    y = _dot(ya[...], w_ref[0:MIX, :])
    y = y + _dot(yb[...], w_ref[MIX:2 * MIX, :])
    y = y + _dot(yc[...], w_ref[2 * MIX:3 * MIX, :])
    y = y + _dot(yd[...], w_ref[3 * MIX:4 * MIX, :])
    x1 = x_ref[...] + g1_ref[0] * y
    x1_ref[...] = x1
    h = x1 * lax.rsqrt(jnp.mean(x1 * x1, axis=-1, keepdims=True) + EPS) * nw_ref[...]
    h = h * (1.0 + sc_ref[0]) + sh_ref[0]
    h2_ref[...] = h.astype(bf16)
    h2t_ref[...] = h.T.astype(bf16)


def out_proj(ys, x, w_out, g1, nw, sc, sh, t_p, l_s):
    t = x.shape[0]
    tm = 512
    row = _mod_row_map(tm, t_p, l_s)
    tok = lambda w: pl.BlockSpec((tm, w), lambda i: (i, 0))
    return pl.pallas_call(
        _out_body,
        grid=(t // tm,),
        in_specs=[tok(MIX), tok(MIX), tok(MIX), tok(MIX), tok(D_MODEL),
                  pl.BlockSpec((D_MODEL, D_MODEL), lambda i: (0, 0)),
                  pl.BlockSpec((1, 1, D_MODEL), row),
                  pl.BlockSpec((1, D_MODEL), lambda i: (0, 0)),
                  pl.BlockSpec((1, 1, D_MODEL), row),
                  pl.BlockSpec((1, 1, D_MODEL), row)],
        out_specs=[tok(D_MODEL), tok(D_MODEL), pl.BlockSpec((D_MODEL, tm), lambda i: (0, i))],
        out_shape=[jax.ShapeDtypeStruct((t, D_MODEL), f32),
                   jax.ShapeDtypeStruct((t, D_MODEL), bf16),
                   jax.ShapeDtypeStruct((D_MODEL, t), bf16)],
        compiler_params=_params("arbitrary"),
        name="out_proj",
    )(*ys, x, w_out, g1, nw, sc, sh)


def _top16(s, n):
    io = lax.broadcasted_iota(jnp.int32, s.shape, 0).astype(f32)
    rank = jnp.full(s.shape, float(PEER_TOPK), f32)
    vals = []
    for r in range(PEER_TOPK):
        m = jnp.max(s, axis=0, keepdims=True)
        idx = jnp.min(jnp.where(s == m, io, float(n)), axis=0, keepdims=True)
        hit = io == idx
        rank = jnp.where(hit, float(r), rank)
        s = jnp.where(hit, -jnp.inf, s)
        vals.append(m)
    return jnp.concatenate(vals, axis=0), rank


def _peer_route_body(h2_ref, wq_ref, keys_ref, r2_ref, e2_ref, lim_ref, cw_ref):
    k = PEER_TOPK
    q = _dot(h2_ref[...], wq_ref[...])
    tn = q.shape[0]
    n_b = [k // (a + 1) for a in range(k)]
    n_cand = sum(n_b)
    n_rows = -(-n_cand // 8) * 8
    ra = lax.broadcasted_iota(jnp.int32, (k, n_rows), 0)
    rc = lax.broadcasted_iota(jnp.int32, (k, n_rows), 1)
    rep_a = jnp.zeros((k, n_rows), f32)
    start = 0
    for a in range(k):
        rep_a = jnp.where((ra == a) & (rc >= start) & (rc < start + n_b[a]), 1.0, rep_a)
        start += n_b[a]
    lw = 128
    pad = jnp.full((n_rows - n_cand, lw), -jnp.inf, f32)
    for h in range(PEER_HEADS):
        s1_all = _dot_nt(keys_ref[h, 0], q[:, h * 256:h * 256 + 128], HI)
        s2_all = _dot_nt(keys_ref[h, 1], q[:, h * 256 + 128:h * 256 + 256], HI)
        for j in range(tn // lw):
            cols = slice(j * lw, (j + 1) * lw)
            s1 = s1_all[:, cols]
            s2 = s2_all[:, cols]
            t1, rank1 = _top16(s1, PEER_NKEYS)
            t2, rank2 = _top16(s2, PEER_NKEYS)
            cand = jnp.concatenate([t1[a:a + 1, :] + t2[0:n_b[a], :] for a in range(k)] + [pad], axis=0)
            best, crank = _top16(cand, n_rows)
            sel = jnp.where(crank < k, 1.0, 0.0).astype(f32)
            count_a = _dot(rep_a, sel)
            zsum = jnp.sum(jnp.exp(best - best[0:1, :]), axis=0, keepdims=True)
            lim = jnp.zeros((PEER_NKEYS, lw), f32)
            for a in range(k):
                lim = jnp.where(rank1 == a, count_a[a:a + 1, :], lim)
            r2_ref[h, :, cols] = rank2.astype(bf16)
            e2_ref[h, :, cols] = jnp.exp(s2 - t2[0:1, :]).astype(bf16)
            lim_ref[h, :, cols] = lim
            cw_ref[h, :, cols] = jnp.exp(s1 - t1[0:1, :]) / zsum


def peer_route(h2, w_q, keys):
    t = h2.shape[0]
    tn = 256
    arr = jax.ShapeDtypeStruct((PEER_HEADS, PEER_NKEYS, t), f32)
    arr_h = jax.ShapeDtypeStruct((PEER_HEADS, PEER_NKEYS, t), bf16)
    spec = pl.BlockSpec((PEER_HEADS, PEER_NKEYS, tn), lambda i: (0, 0, i))
    return pl.pallas_call(
        _peer_route_body,
        grid=(t // tn,),
        in_specs=[pl.BlockSpec((tn, D_MODEL), lambda i: (i, 0)),
                  pl.BlockSpec((D_MODEL, 2 * D_MODEL), lambda i: (0, 0)),
                  pl.BlockSpec((PEER_HEADS, 2, PEER_NKEYS, PEER_NKEYS), lambda i: (0, 0, 0, 0))],
        out_specs=[spec, spec, spec, spec],
        out_shape=[arr_h, arr_h, arr, arr],
        compiler_params=_params("arbitrary"),
        name="peer_route",
    )(h2, w_q, keys)


def _peer_expert_body(eb_i, h2t_ref, u_ref, v_ref, r2_ref, e2_ref, lim_ref, cw_ref, x1_ref, g2_ref,
                      o_ref, wt, acc):
    e = pl.program_id(1)

    @pl.when(e == 0)
    def _():
        acc[...] = jnp.zeros_like(acc)

    sub_i = 4
    sub = sub_i * PEER_NKEYS
    n_sub = eb_i // sub_i

    def scores(c):
        return _dot(u_ref[pl.ds(c * sub, sub), :], h2t_ref[...])

    st_next = scores(0)
    for c in range(n_sub):
        rows = pl.ds(c * sub, sub)
        st = st_next
        if c + 1 < n_sub:
            st_next = scores(c + 1)
        for ii in range(sub_i):
            i = e * eb_i + c * sub_i + ii
            g = None
            for h in range(PEER_HEADS):
                lim = lim_ref[h, pl.ds(i, 1), :].astype(bf16)
                cw = cw_ref[h, pl.ds(i, 1), :].astype(bf16)
                term = jnp.where(r2_ref[h] < lim, e2_ref[h] * cw, jnp.zeros((), bf16))
                g = term if g is None else g + term
            s = st[ii * PEER_NKEYS:(ii + 1) * PEER_NKEYS, :].astype(bf16)
            act = (0.5 * s) * (1.0 + lax.erf(s * (2.0 ** -0.5)))
            wt[pl.ds(c * sub + ii * PEER_NKEYS, PEER_NKEYS), :] = g * act
        acc[...] += _dot_tn(wt[rows, :], v_ref[rows, :])

    @pl.when(e == pl.num_programs(1) - 1)
    def _():
        o_ref[...] = x1_ref[...] + g2_ref[0] * acc[...]


def peer_experts(h2t, u_tab, v_tab, route, x1, g2, t_p, l_s):
    t = x1.shape[0]
    tn = 512
    eb_i = 16
    eb = eb_i * PEER_NKEYS
    row = _mod_row_map(tn, t_p, l_s)
    rspec = pl.BlockSpec((PEER_HEADS, PEER_NKEYS, tn), lambda j, e: (0, 0, j))
    return pl.pallas_call(
        functools.partial(_peer_expert_body, eb_i),
        grid=(t // tn, PEER_EXPERTS // eb),
        in_specs=[pl.BlockSpec((D_MODEL, tn), lambda j, e: (0, j)),
                  pl.BlockSpec((eb, D_MODEL), lambda j, e: (e, 0)),
                  pl.BlockSpec((eb, D_MODEL), lambda j, e: (e, 0)),
                  rspec, rspec, rspec, rspec,
                  pl.BlockSpec((tn, D_MODEL), lambda j, e: (j, 0)),
                  pl.BlockSpec((1, 1, D_MODEL), lambda j, e: row(j))],
        out_specs=pl.BlockSpec((tn, D_MODEL), lambda j, e: (j, 0)),
        out_shape=jax.ShapeDtypeStruct((t, D_MODEL), f32),
        scratch_shapes=[pltpu.VMEM((eb, tn), bf16), pltpu.VMEM((tn, D_MODEL), f32)],
        compiler_params=_params("arbitrary", "arbitrary"),
        name="peer_experts",
    )(h2t, u_tab, v_tab, *route, x1, g2)


def _regroup_w_in(w):
    o_gdn = 776
    o_fn = o_gdn + 1040
    o_att = o_fn + 256
    main = jnp.concatenate([w[:, 0:768], w[:, o_gdn:o_gdn + 1024], w[:, o_fn:o_fn + 256], w[:, o_att:o_att + 512]], axis=1)
    small = jnp.concatenate([w[:, 768:776], w[:, o_gdn + 1024:o_gdn + 1040],
                             jnp.zeros((w.shape[0], U_SM - 24), w.dtype)], axis=1)
    return main.astype(bf16), small, small.T


def _layer_params(i, ssd_conv_w, ssd_conv_b, ssd_A_log, ssd_dt_bias, ssd_D, ssd_norm_w,
                  gdn_conv_w, gdn_conv_b, gdn_A_log, gdn_dt_bias, gdn_norm_w, q_norm_w, k_norm_w, att_sink):
    ssd = dict(conv_w=ssd_conv_w[i], conv_b=ssd_conv_b[i][None, :],
               al_row=ssd_A_log[i].reshape(1, 8), al_col=ssd_A_log[i].reshape(8, 1),
               db_row=ssd_dt_bias[i].reshape(1, 8), db_col=ssd_dt_bias[i].reshape(8, 1),
               d_skip=jnp.repeat(ssd_D[i], HEAD_DIM)[None, :], norm_w=ssd_norm_w[i][None, :])
    gdn = dict(conv_w=gdn_conv_w[i], conv_b=gdn_conv_b[i][None, :],
               al_row=gdn_A_log[i].reshape(1, 8), al_col=gdn_A_log[i].reshape(8, 1),
               db_row=gdn_dt_bias[i].reshape(1, 8), db_col=gdn_dt_bias[i].reshape(8, 1),
               norm_w=jnp.tile(gdn_norm_w[i], N_HEADS)[None, :])
    att = dict(q_norm=jnp.tile(q_norm_w[i], N_HEADS)[None, :], k_norm=jnp.tile(k_norm_w[i], 2)[None, :],
               sink=att_sink[i][None, :])
    return ssd, gdn, att


def kernel(x_prompt, x_sample, cache_k, cache_v, state_ssd, state_gdn, c, c_ctx, w_mod, b_mod, norm1_w, norm2_w, w_in, w_out, ssd_conv_w, ssd_conv_b, ssd_A_log, ssd_dt_bias, ssd_D, ssd_norm_w, gdn_conv_w, gdn_conv_b, gdn_A_log, gdn_dt_bias, gdn_norm_w, q_norm_w, k_norm_w, att_sink, peer_w_q, peer_keys, peer_u, peer_v):
    nb_p, l_p, d = x_prompt.shape
    nb_s, l_s, _ = x_sample.shape
    past = cache_k.shape[2]
    t_p = nb_p * l_p
    x = jnp.concatenate([x_prompt.reshape(t_p, d), x_sample.reshape(nb_s * l_s, d)], axis=0)

    cvecs = jnp.concatenate([c_ctx[None, :], c, jnp.zeros((8 - 1 - nb_s, d), f32)], axis=0)
    mods = adaln_all(cvecs, w_mod, b_mod)
    dft_p, dft_s = dft_tables(l_p), dft_tables(l_s)
    rope = rope_tables(l_s)
    zeros_state = jnp.zeros((nb_p, 2, N_HEADS, HEAD_DIM, HEAD_DIM), f32)

    ks, vs, ssd_states, gdn_states = [], [], [], []
    for i in range(DEPTH):
        sh1, sc1, g1, sh2, sc2, g2 = [mods[i, :, j * d:(j + 1) * d].reshape(8, 1, d) for j in range(6)]
        ssd_p, gdn_p, att_p = _layer_params(i, ssd_conv_w, ssd_conv_b, ssd_A_log, ssd_dt_bias, ssd_D, ssd_norm_w,
                                            gdn_conv_w, gdn_conv_b, gdn_A_log, gdn_dt_bias, gdn_norm_w,
                                            q_norm_w, k_norm_w, att_sink)
        w_main, w_sm, w_smt = _regroup_w_in(w_in[i])
        u_ssd, u_gdn, u_fn, u_att, u_sm, u_smt = in_proj(x, norm1_w[i][None, :], sc1, sh1, w_main, w_sm, w_smt, t_p, l_s)

        y_ssd_p, st_ssd = ssd_mixer(u_ssd, u_sm, u_smt, 0, nb_p, l_p, ssd_p, zeros_state)
        y_ssd_s, _ = ssd_mixer(u_ssd, u_sm, u_smt, t_p, nb_s, l_s, ssd_p, state_ssd[:, i])
        y_gdn_p, st_gdn = gdn_mixer(u_gdn, u_sm, u_smt, 0, nb_p, l_p, gdn_p, zeros_state)
        y_gdn_s, _ = gdn_mixer(u_gdn, u_sm, u_smt, t_p, nb_s, l_s, gdn_p, state_gdn[:, i])
        y_fn_p = fnet_mixer(u_fn, 0, nb_p, l_p, dft_p)
        y_fn_s = fnet_mixer(u_fn, t_p, nb_s, l_s, dft_s)
        y_att_p, k_new, v_new = ctx_attention(u_att, 0, nb_p, l_p, att_p)
        y_att_s = latent_attention(u_att, t_p, nb_s, l_s, att_p,
                                   cache_k[:, i].reshape(nb_s, past, 2 * HEAD_DIM),
                                   cache_v[:, i].reshape(nb_s, past, 2 * HEAD_DIM), rope)
        ys = [jnp.concatenate([a, b], axis=0) for a, b in
              ((y_ssd_p, y_ssd_s), (y_gdn_p, y_gdn_s), (y_fn_p, y_fn_s), (y_att_p, y_att_s))]

        x1, h2, h2t = out_proj(ys, x, w_out[i].astype(bf16), g1, norm2_w[i][None, :], sc2, sh2, t_p, l_s)
        route = peer_route(h2, peer_w_q[i].astype(bf16), peer_keys[i])
        x = peer_experts(h2t, peer_u[i].astype(bf16), peer_v[i].astype(bf16), route, x1, g2, t_p, l_s)

        ks.append(k_new.reshape(nb_p, l_p, 2, HEAD_DIM))
        vs.append(v_new.reshape(nb_p, l_p, 2, HEAD_DIM))
        ssd_states.append(st_ssd)
        gdn_states.append(st_gdn)

    return (x[:t_p].reshape(nb_p, l_p, d), x[t_p:].reshape(nb_s, l_s, d),
            jnp.stack(ks, axis=1), jnp.stack(vs, axis=1),
            jnp.stack(ssd_states, axis=1), jnp.stack(gdn_states, axis=1))
```

```python
import functools
import math

import jax
import jax.numpy as jnp
from jax import lax
from jax.experimental import pallas as pl
from jax.experimental.pallas import tpu as pltpu

f32 = jnp.float32
bf16 = jnp.bfloat16
HI = lax.Precision.HIGHEST

D_MODEL = 1024
DEPTH = 4
EPS = 1e-6
NEG_INF = -1e30
CONV_W = 5
HEAD_DIM = 64
N_HEADS = 4
MIX = 256
SSD_CHUNK = 256
GDN_CHUNK = 64
WINDOW = 128
ABLOCK = 128
GRID_W = 64
ROPE_THETA = 10000.0
PEER_HEADS = 8
PEER_NKEYS = 128
PEER_TOPK = 16
PEER_EXPERTS = PEER_NKEYS * PEER_NKEYS

VMEM_LIMIT = 56 * 1024 * 1024

U_SSD, U_GDN, U_FN, U_ATT, U_SM = 768, 1024, 256, 512, 128
U_MAIN = U_SSD + U_GDN + U_FN + U_ATT


def _dot(a, b, prec=None):
    return jnp.dot(a, b, preferred_element_type=f32, precision=prec)


def _dot_nt(a, b, prec=None):
    return lax.dot_general(a, b, (((1,), (1,)), ((), ())), preferred_element_type=f32, precision=prec)


def _dot_tn(a, b, prec=None):
    return lax.dot_general(a, b, (((0,), (0,)), ((), ())), preferred_element_type=f32, precision=prec)


def _split_bf16(x):
    hi = x.astype(bf16)
    return hi, (x - hi.astype(f32)).astype(bf16)


def _dot3(a, b):
    ah, al = _split_bf16(a)
    bh, bl = _split_bf16(b)
    return _dot(ah, bh) + (_dot(ah, bl) + _dot(al, bh))


def _dotb(a, b):
    return _dot(a.astype(bf16), b.astype(bf16))


def _sigmoid(x):
    return 1.0 / (1.0 + jnp.exp(-x))


def _silu(x):
    return x * _sigmoid(x)


def _softplus(x):
    return jnp.maximum(x, 0.0) + jnp.log(1.0 + jnp.exp(-jnp.abs(x)))


def _params(*sem):
    return pltpu.CompilerParams(dimension_semantics=sem, vmem_limit_bytes=VMEM_LIMIT)


def _idiv(x, n):
    return lax.shift_right_logical(x, int(math.log2(n)))


def _head_mean_matrix(width):
    r = _idiv(lax.broadcasted_iota(jnp.int32, (width, width), 0), HEAD_DIM)
    c = _idiv(lax.broadcasted_iota(jnp.int32, (width, width), 1), HEAD_DIM)
    return jnp.where(r == c, 1.0 / HEAD_DIM, 0.0).astype(f32)


def _shift_rows(x, d, n):
    row = lax.broadcasted_iota(jnp.int32, x.shape, 0)
    y = pltpu.roll(x, (-d) % n, 0)
    ok = (row + d >= 0) & (row + d < n)
    return jnp.where(ok, y, 0.0)


def _conv_silu(x, w_ref, b_ref, n):
    half = (CONV_W - 1) // 2
    acc = b_ref[...] + w_ref[half:half + 1, :] * x
    for k in range(CONV_W):
        if k != half:
            acc = acc + w_ref[k:k + 1, :] * _shift_rows(x, k - half, n)
    return _silu(acc)


def _mod_body(c_ref, w_ref, b_ref, o_ref):
    o_ref[0] = _dot(_silu(c_ref[...]), w_ref[0], HI) + b_ref[0]


def adaln_all(cvecs, w_mod, b_mod):
    tn = 1536
    n = w_mod.shape[-1]
    return pl.pallas_call(
        _mod_body,
        grid=(DEPTH, n // tn),
        in_specs=[pl.BlockSpec((8, D_MODEL), lambda l, j: (0, 0)),
                  pl.BlockSpec((1, D_MODEL, tn), lambda l, j: (l, 0, j)),
                  pl.BlockSpec((1, 1, tn), lambda l, j: (l, 0, j))],
        out_specs=pl.BlockSpec((1, 8, tn), lambda l, j: (l, 0, j)),
        out_shape=jax.ShapeDtypeStruct((DEPTH, 8, n), f32),
        compiler_params=_params("arbitrary", "arbitrary"),
        name="adaln",
    )(cvecs, w_mod, b_mod.reshape(DEPTH, 1, n))


def _in_body(x_ref, nw_ref, sc_ref, sh_ref, w_ref, ws_ref, wst_ref,
             o_ssd, o_gdn, o_fn, o_att, o_sm, o_smt):
    x = x_ref[...]
    h = x * lax.rsqrt(jnp.mean(x * x, axis=-1, keepdims=True) + EPS) * nw_ref[...]
    h = h * (1.0 + sc_ref[0]) + sh_ref[0]
    u = _dot(h.astype(bf16), w_ref[...])
    o_ssd[...] = u[:, 0:U_SSD]
    o_gdn[...] = u[:, U_SSD:U_SSD + U_GDN]
    o_fn[...] = u[:, U_SSD + U_GDN:U_SSD + U_GDN + U_FN]
    o_att[...] = u[:, U_SSD + U_GDN + U_FN:U_MAIN]
    o_sm[...] = _dot(h, ws_ref[...], HI)
    o_smt[...] = _dot_nt(wst_ref[...], h, HI)


def _mod_row_map(tm, t_p, l_s):
    def f(i):
        t0 = i * tm
        return (jnp.where(t0 < t_p, 0, 1 + (t0 - t_p) // l_s), 0, 0)
    return f


def in_proj(x, nw, sc, sh, w_main, w_sm, w_smt, t_p, l_s):
    t = x.shape[0]
    tm = 512
    row = _mod_row_map(tm, t_p, l_s)
    widths = (U_SSD, U_GDN, U_FN, U_ATT, U_SM)
    return pl.pallas_call(
        _in_body,
        grid=(t // tm,),
        in_specs=[pl.BlockSpec((tm, D_MODEL), lambda i: (i, 0)),
                  pl.BlockSpec((1, D_MODEL), lambda i: (0, 0)),
                  pl.BlockSpec((1, 1, D_MODEL), row),
                  pl.BlockSpec((1, 1, D_MODEL), row),
                  pl.BlockSpec((D_MODEL, U_MAIN), lambda i: (0, 0)),
                  pl.BlockSpec((D_MODEL, U_SM), lambda i: (0, 0)),
                  pl.BlockSpec((U_SM, D_MODEL), lambda i: (0, 0))],
        out_specs=[pl.BlockSpec((tm, w), lambda i: (i, 0)) for w in widths]
        + [pl.BlockSpec((U_SM, tm), lambda i: (0, i))],
        out_shape=[jax.ShapeDtypeStruct((t, w), f32) for w in widths]
        + [jax.ShapeDtypeStruct((U_SM, t), f32)],
        compiler_params=_params("arbitrary"),
        name="in_proj",
    )(x, nw, sc, sh, w_main, w_sm, w_smt)


def _ssd_body(L, u_ref, sm_ref, smt_ref, cw_ref, cb_ref, alr_ref, alc_ref, dbr_ref, dbc_ref,
              dsk_ref, nw_ref, h0_ref, y_ref, hT_ref, yacc):
    Q = min(L, SSD_CHUNK)
    nc = L // Q
    hd = HEAD_DIM
    z = u_ref[:, 0:MIX]
    xbc = _conv_silu(u_ref[:, MIX:U_SSD], cw_ref, cb_ref, L)
    xs = xbc[:, 0:MIX]
    dtc = _softplus(sm_ref[:, 0:8] + dbr_ref[...])
    dtr = _softplus(smt_ref[0:8, :] + dbc_ref[...])
    ac = dtc * (-jnp.exp(alr_ref[...]))
    ar = dtr * (-jnp.exp(alc_ref[...]))
    ri = lax.broadcasted_iota(jnp.int32, (Q, Q), 0)
    ci = lax.broadcasted_iota(jnp.int32, (Q, Q), 1)
    low = ri >= ci
    upp = ci >= ri
    tril = jnp.where(low, 1.0, 0.0).astype(f32)

    def chunk_terms(c):
        r0 = c * Q
        a_c = ac[r0:r0 + Q, :]
        a_r = ar[:, r0:r0 + Q]
        pc = _dot(tril, a_c, HI)
        pr = _dot_nt(a_r, tril, HI)
        return a_c, a_r, pc, pr

    hf = [h0_ref[0, 0, h] for h in range(N_HEADS)]
    for c in range(nc):
        r0 = c * Q
        a_c, a_r, pc, pr = chunk_terms(c)
        ys = []
        for h in range(N_HEADS):
            g = h // 2
            x_h = xs[r0:r0 + Q, h * hd:(h + 1) * hd]
            b_g = xbc[r0:r0 + Q, MIX + g * hd:MIX + (g + 1) * hd]
            c_g = xbc[r0:r0 + Q, MIX + 2 * hd + g * hd:MIX + 2 * hd + (g + 1) * hd]
            gm = _dot_nt(c_g, b_g, HI)
            hb = N_HEADS + h
            seg_f = pc[:, h:h + 1] - pr[h:h + 1, :]
            l_f = jnp.where(low, jnp.exp(jnp.minimum(seg_f, 0.0)), 0.0)
            e_c = pc[:, hb:hb + 1] - a_c[:, hb:hb + 1]
            e_r = pr[hb:hb + 1, :] - a_r[hb:hb + 1, :]
            seg_b = e_r - e_c
            l_b = jnp.where(upp, jnp.exp(jnp.minimum(seg_b, 0.0)), 0.0)
            m = gm * (l_f * dtr[h:h + 1, r0:r0 + Q] + l_b * dtr[hb:hb + 1, r0:r0 + Q])
            y_h = _dot(m, x_h, HI)
            y_h = y_h + _dot_nt(c_g * jnp.exp(pc[:, h:h + 1]), hf[h], HI)
            tot = pc[Q - 1:Q, h:h + 1]
            wgt = dtc[r0:r0 + Q, h:h + 1] * jnp.exp(tot - pc[:, h:h + 1])
            hf[h] = hf[h] * jnp.exp(tot) + _dot_tn(x_h * wgt, b_g, HI)
            ys.append(y_h)
        yacc[r0:r0 + Q, :] = jnp.concatenate(ys, axis=1)
    hb_s = [h0_ref[0, 1, h] for h in range(N_HEADS)]
    for c in range(nc - 1, -1, -1):
        r0 = c * Q
        a_c, a_r, pc, pr = chunk_terms(c)
        ys = []
        for h in range(N_HEADS):
            g = h // 2
            hb = N_HEADS + h
            x_h = xs[r0:r0 + Q, h * hd:(h + 1) * hd]
            b_g = xbc[r0:r0 + Q, MIX + g * hd:MIX + (g + 1) * hd]
            c_g = xbc[r0:r0 + Q, MIX + 2 * hd + g * hd:MIX + 2 * hd + (g + 1) * hd]
            e_c = pc[:, hb:hb + 1] - a_c[:, hb:hb + 1]
            tot = pc[Q - 1:Q, hb:hb + 1]
            ys.append(_dot_nt(c_g * jnp.exp(tot - e_c), hb_s[h], HI))
            wgt = dtc[r0:r0 + Q, hb:hb + 1] * jnp.exp(e_c)
            hb_s[h] = hb_s[h] * jnp.exp(tot) + _dot_tn(x_h * wgt, b_g, HI)
        yacc[r0:r0 + Q, :] = yacc[r0:r0 + Q, :] + jnp.concatenate(ys, axis=1)
    for h in range(N_HEADS):
        hT_ref[0, 0, h] = hf[h]
        hT_ref[0, 1, h] = hb_s[h]
    y = yacc[...] + dsk_ref[...] * xs
    y = y * _silu(z)
    ms = _dot(y * y, _head_mean_matrix(MIX), HI)
    y_ref[...] = (y * lax.rsqrt(ms + EPS) * nw_ref[...]).astype(y_ref.dtype)


def ssd_mixer(u_ssd, u_sm, u_smt, t0, nseq, L, p, h0):
    b0 = t0 // L
    small = lambda s: pl.BlockSpec(s, lambda b: (0,) * len(s))
    return pl.pallas_call(
        functools.partial(_ssd_body, L),
        grid=(nseq,),
        in_specs=[pl.BlockSpec((L, U_SSD), lambda b: (b0 + b, 0)),
                  pl.BlockSpec((L, U_SM), lambda b: (b0 + b, 0)),
                  pl.BlockSpec((U_SM, L), lambda b: (0, b0 + b)),
                  small((CONV_W, 512)), small((1, 512)),
                  small((1, 8)), small((8, 1)), small((1, 8)), small((8, 1)),
                  small((1, MIX)), small((1, MIX)),
                  pl.BlockSpec((1, 2, N_HEADS, HEAD_DIM, HEAD_DIM), lambda b: (b, 0, 0, 0, 0))],
        out_specs=[pl.BlockSpec((L, MIX), lambda b: (b, 0)),
                   pl.BlockSpec((1, 2, N_HEADS, HEAD_DIM, HEAD_DIM), lambda b: (b, 0, 0, 0, 0))],
        out_shape=[jax.ShapeDtypeStruct((nseq * L, MIX), bf16),
                   jax.ShapeDtypeStruct((nseq, 2, N_HEADS, HEAD_DIM, HEAD_DIM), f32)],
        scratch_shapes=[pltpu.VMEM((L, MIX), f32)],
        compiler_params=_params("arbitrary"),
        name=f"ssd_{L}",
    )(u_ssd, u_sm, u_smt, p["conv_w"], p["conv_b"], p["al_row"], p["al_col"], p["db_row"], p["db_col"],
      p["d_skip"], p["norm_w"], h0)


def _gdn_body(L, u_ref, sm_ref, smt_ref, cw_ref, cb_ref, alr_ref, alc_ref, dbr_ref, dbc_ref,
              nw_ref, s0_ref, y_ref, sT_ref, qkv, oacc):
    Q = GDN_CHUNK
    nc = L // Q
    W = MIX
    qkv[...] = _conv_silu(u_ref[:, 0:3 * MIX], cw_ref, cb_ref, L)
    hm = _head_mean_matrix(MIX) * float(HEAD_DIM)
    q_all = qkv[:, 0:MIX]
    k_all = qkv[:, MIX:2 * MIX]
    qkv[:, 0:MIX] = q_all * lax.rsqrt(_dot(q_all * q_all, hm, HI) + EPS) * (HEAD_DIM ** -0.5)
    qkv[:, MIX:2 * MIX] = k_all * lax.rsqrt(_dot(k_all * k_all, hm, HI) + EPS)
    lac = -jnp.exp(alr_ref[...]) * _softplus(sm_ref[:, 8:16] + dbr_ref[...])
    lar = -jnp.exp(alc_ref[...]) * _softplus(smt_ref[8:16, :] + dbc_ref[...])
    btc = _sigmoid(sm_ref[:, 16:24])
    btr = _sigmoid(smt_ref[16:24, :])

    ri = lax.broadcasted_iota(jnp.int32, (W, W), 0)
    ci = lax.broadcasted_iota(jnp.int32, (W, W), 1)
    same_head = _idiv(ri, Q) == _idiv(ci, Q)
    blk16 = _idiv(ri, 16) == _idiv(ci, 16)
    eye = jnp.where(ri == ci, 1.0, 0.0).astype(f32)
    rq = lax.broadcasted_iota(jnp.int32, (Q, Q), 0)
    cq = lax.broadcasted_iota(jnp.int32, (Q, Q), 1)
    tril_q = jnp.where(rq >= cq, 1.0, 0.0).astype(f32)
    triu_q = jnp.where(cq >= rq, 1.0, 0.0).astype(f32)

    def stack_cols(m, base):
        return jnp.concatenate([m[:, base + h:base + h + 1] for h in range(N_HEADS)], axis=0)

    def stack_rows(m, base):
        return jnp.concatenate([m[base + h:base + h + 1, :] for h in range(N_HEADS)], axis=1)

    def block_diag(m):
        return jnp.where(same_head, jnp.concatenate([m] * N_HEADS, axis=1), 0.0)

    def heads_to_rows(m):
        return jnp.concatenate([m[:, h * HEAD_DIM:(h + 1) * HEAD_DIM] for h in range(N_HEADS)], axis=0)

    def rows_to_heads(m):
        return jnp.concatenate([m[h * Q:(h + 1) * Q, :] for h in range(N_HEADS)], axis=1)

    def sweep(d, S):
        tri_mat = tril_q if d == 0 else triu_q
        causal = (ri >= ci) if d == 0 else (ci >= ri)
        strict = (ri > ci) if d == 0 else (ci > ri)
        order = range(nc) if d == 0 else range(nc - 1, -1, -1)
        for c in order:
            r0 = c * Q
            qs = heads_to_rows(qkv[r0:r0 + Q, 0:MIX])
            ks = heads_to_rows(qkv[r0:r0 + Q, MIX:2 * MIX])
            vs = heads_to_rows(qkv[r0:r0 + Q, 2 * MIX:3 * MIX])
            la_c = lac[r0:r0 + Q, :]
            la_r = lar[:, r0:r0 + Q]
            gc = stack_cols(_dot(tri_mat, la_c, HI), 4 * d)
            gr = stack_rows(_dot_nt(la_r, tri_mat, HI), 4 * d)
            bc = stack_cols(btc[r0:r0 + Q, :], 4 * d)
            ends = []
            for h in range(N_HEADS):
                e = gr[:, h * Q + Q - 1:h * Q + Q] if d == 0 else gr[:, h * Q:h * Q + 1]
                ends.append(jnp.broadcast_to(e, (Q, 1)))
            g_end = jnp.concatenate(ends, axis=0)
            decay = jnp.where(causal & same_head, jnp.exp(jnp.minimum(gc - gr, 0.0)), 0.0)
            kbd = block_diag(ks).astype(bf16)
            kk = _dot_nt(kbd, kbd)
            a = jnp.where(strict, bc * kk * decay, 0.0)
            n = jnp.where(blk16, a, 0.0)
            n2 = _dotb(n, n)
            n4 = _dotb(n2, n2)
            n8 = _dotb(n4, n4)
            dinv = _dotb(_dotb(eye - n, eye + n2), _dotb(eye + n4, eye + n8))
            zz = _dotb(dinv, a - n)
            z2 = _dotb(zz, zz)
            tinv = _dotb(_dotb(eye - zz, eye + z2), dinv)
            rhs = jnp.concatenate([vs * bc, ks * (bc * jnp.exp(gc))], axis=1)
            sol = _dotb(tinv, rhs)
            u_c = sol[:, 0:HEAD_DIM]
            w_c = sol[:, HEAD_DIM:2 * HEAD_DIM]
            s_b = S.astype(bf16)
            attn = jnp.where(causal, _dot_nt(block_diag(qs).astype(bf16), kbd) * decay, 0.0)
            v_new = u_c - _dot(block_diag(w_c).astype(bf16), s_b)
            v_b = v_new.astype(bf16)
            o = _dot(block_diag(qs * jnp.exp(gc)).astype(bf16), s_b) + _dot(attn.astype(bf16), v_b)
            S = S * jnp.exp(g_end) + _dot_tn(block_diag(ks * jnp.exp(g_end - gc)).astype(bf16), v_b)
            o = rows_to_heads(o)
            if d == 0:
                oacc[r0:r0 + Q, :] = o
            else:
                oacc[r0:r0 + Q, :] = oacc[r0:r0 + Q, :] + o
        return S

    for d in range(2):
        S0 = jnp.concatenate([s0_ref[0, d, h] for h in range(N_HEADS)], axis=0)
        S = sweep(d, S0)
        for h in range(N_HEADS):
            sT_ref[0, d, h] = S[h * HEAD_DIM:(h + 1) * HEAD_DIM, :]
    o = oacc[...]
    ms = _dot(o * o, _head_mean_matrix(MIX), HI)
    o = o * lax.rsqrt(ms + EPS) * nw_ref[...]
    y_ref[...] = (o * _silu(u_ref[:, 3 * MIX:4 * MIX])).astype(y_ref.dtype)


def gdn_mixer(u_gdn, u_sm, u_smt, t0, nseq, L, p, s0):
    b0 = t0 // L
    small = lambda s: pl.BlockSpec(s, lambda b: (0,) * len(s))
    return pl.pallas_call(
        functools.partial(_gdn_body, L),
        grid=(nseq,),
        in_specs=[pl.BlockSpec((L, U_GDN), lambda b: (b0 + b, 0)),
                  pl.BlockSpec((L, U_SM), lambda b: (b0 + b, 0)),
                  pl.BlockSpec((U_SM, L), lambda b: (0, b0 + b)),
                  small((CONV_W, 3 * MIX)), small((1, 3 * MIX)),
                  small((1, 8)), small((8, 1)), small((1, 8)), small((8, 1)),
                  small((1, MIX)),
                  pl.BlockSpec((1, 2, N_HEADS, HEAD_DIM, HEAD_DIM), lambda b: (b, 0, 0, 0, 0))],
        out_specs=[pl.BlockSpec((L, MIX), lambda b: (b, 0)),
                   pl.BlockSpec((1, 2, N_HEADS, HEAD_DIM, HEAD_DIM), lambda b: (b, 0, 0, 0, 0))],
        out_shape=[jax.ShapeDtypeStruct((nseq * L, MIX), bf16),
                   jax.ShapeDtypeStruct((nseq, 2, N_HEADS, HEAD_DIM, HEAD_DIM), f32)],
        scratch_shapes=[pltpu.VMEM((L, 3 * MIX), f32), pltpu.VMEM((L, MIX), f32)],
        compiler_params=_params("arbitrary"),
        name=f"gdn_{L}",
    )(u_gdn, u_sm, u_smt, p["conv_w"], p["conv_b"], p["al_row"], p["al_col"], p["db_row"], p["db_col"],
      p["norm_w"], s0)


def _fnet_body(x_ref, cl_ref, sl_ref, cc_ref, sc_ref, y_ref):
    x = x_ref[...]
    xc = _dot(x, cc_ref[...], HI)
    xs = _dot(x, sc_ref[...], HI)
    y_ref[...] = (_dot(cl_ref[...], xc, HI) - _dot(sl_ref[...], xs, HI)).astype(y_ref.dtype)


def fnet_mixer(u_fn, t0, nseq, L, tabs):
    b0 = t0 // L
    full = lambda s: pl.BlockSpec(s, lambda b: (0,) * len(s))
    return pl.pallas_call(
        _fnet_body,
        grid=(nseq,),
        in_specs=[pl.BlockSpec((L, MIX), lambda b: (b0 + b, 0)),
                  full((L, L)), full((L, L)), full((MIX, MIX)), full((MIX, MIX))],
        out_specs=pl.BlockSpec((L, MIX), lambda b: (b, 0)),
        out_shape=jax.ShapeDtypeStruct((nseq * L, MIX), bf16),
        compiler_params=_params("arbitrary"),
        name=f"fnet_{L}",
    )(u_fn, *tabs)


def dft_tables(L):
    n = jnp.arange(L, dtype=jnp.int32)
    ang_l = (2.0 * math.pi / L) * ((n[:, None] * n[None, :]) % L).astype(f32)
    m = jnp.arange(MIX, dtype=jnp.int32)
    same = (m[:, None] // HEAD_DIM) == (m[None, :] // HEAD_DIM)
    ang_c = (2.0 * math.pi / HEAD_DIM) * (((m[:, None] % HEAD_DIM) * (m[None, :] % HEAD_DIM)) % HEAD_DIM).astype(f32)
    sl = 1.0 / math.sqrt(L)
    sc = 1.0 / math.sqrt(HEAD_DIM)
    return (jnp.cos(ang_l) * sl, jnp.sin(ang_l) * sl,
            jnp.where(same, jnp.cos(ang_c) * sc, 0.0), jnp.where(same, jnp.sin(ang_c) * sc, 0.0))


def _qk_norm(x, w_row, width):
    ms = _dot(x * x, _head_mean_matrix(width), HI)
    return x * lax.rsqrt(ms + EPS) * w_row


def _sink_softmax_pv(s_list, v_list, sink):
    m = sink
    for s in s_list:
        m = jnp.maximum(m, jnp.max(s, axis=-1, keepdims=True))
    den = jnp.exp(sink - m)
    acc = None
    for s, v in zip(s_list, v_list):
        e = jnp.exp(s - m)
        den = den + jnp.sum(e, axis=-1, keepdims=True)
        pv = _dotb(e, v)
        acc = pv if acc is None else acc + pv
    return acc / den


def _ctx_att_body(L, u_ref, qw_ref, kw_ref, sink_ref, y_ref, k_ref, v_ref):
    hd = HEAD_DIM
    scale = hd ** -0.5
    q = _qk_norm(u_ref[:, 0:MIX], qw_ref[...], MIX)
    k = _qk_norm(u_ref[:, MIX:MIX + 2 * hd], kw_ref[...], 2 * hd)
    v = u_ref[:, MIX + 2 * hd:MIX + 4 * hd]
    k_ref[...] = k
    v_ref[...] = v
    outs = []
    for g in range(2):
        k_g = k[:, g * hd:(g + 1) * hd]
        v_g = v[:, g * hd:(g + 1) * hd]
        for r in range(2):
            h = 2 * g + r
            s = _dot_nt(q[:, h * hd:(h + 1) * hd].astype(bf16), k_g.astype(bf16)) * scale
            outs.append(_sink_softmax_pv([s], [v_g], sink_ref[0:1, h:h + 1]))
    y_ref[...] = jnp.concatenate(outs, axis=1).astype(y_ref.dtype)


def ctx_attention(u_att, t0, nseq, L, p):
    b0 = t0 // L
    small = lambda s: pl.BlockSpec(s, lambda b: (0,) * len(s))
    return pl.pallas_call(
        functools.partial(_ctx_att_body, L),
        grid=(nseq,),
        in_specs=[pl.BlockSpec((L, U_ATT), lambda b: (b0 + b, 0)),
                  small((1, MIX)), small((1, 2 * HEAD_DIM)), small((1, N_HEADS))],
        out_specs=[pl.BlockSpec((L, MIX), lambda b: (b, 0)),
                   pl.BlockSpec((L, 2 * HEAD_DIM), lambda b: (b, 0)),
                   pl.BlockSpec((L, 2 * HEAD_DIM), lambda b: (b, 0))],
        out_shape=[jax.ShapeDtypeStruct((nseq * L, MIX), bf16),
                   jax.ShapeDtypeStruct((nseq * L, 2 * HEAD_DIM), f32),
                   jax.ShapeDtypeStruct((nseq * L, 2 * HEAD_DIM), f32)],
        compiler_params=_params("arbitrary"),
        name="ctx_att",
    )(u_att, p["q_norm"], p["k_norm"], p["sink"])


def _rope(x, cos_ref, sin_ref, width):
    lane = lax.broadcasted_iota(jnp.int32, x.shape, 1) & (HEAD_DIM - 1)
    half = HEAD_DIM // 2
    swapped = jnp.where(lane < half, pltpu.roll(x, width - half, 1), pltpu.roll(x, half, 1))
    return x * cos_ref[...] + swapped * sin_ref[...]


def _lat_att_body(L, P, u_ref, kc_ref, vc_ref, qw_ref, kw_ref, sink_ref, cq_ref, sq_ref, ck_ref, sk_ref, y_ref):
    hd = HEAD_DIM
    scale = hd ** -0.5
    nb = L // ABLOCK
    q = _rope(_qk_norm(u_ref[:, 0:MIX], qw_ref[...], MIX), cq_ref, sq_ref, MIX)
    k = _rope(_qk_norm(u_ref[:, MIX:MIX + 2 * hd], kw_ref[...], 2 * hd), ck_ref, sk_ref, 2 * hd)
    v = u_ref[:, MIX + 2 * hd:MIX + 4 * hd]
    kc = kc_ref[0]
    vc = vc_ref[0]
    for i in range(nb):
        lo = max(i - 1, 0) * ABLOCK
        hi = min(i + 2, nb) * ABLOCK
        qpos = i * ABLOCK + lax.broadcasted_iota(jnp.int32, (ABLOCK, hi - lo), 0)
        kpos = lo + lax.broadcasted_iota(jnp.int32, (ABLOCK, hi - lo), 1)
        dist = qpos - kpos
        ok = (dist <= WINDOW) & (dist >= -WINDOW)
        outs = []
        for g in range(2):
            k_l = k[lo:hi, g * hd:(g + 1) * hd]
            v_l = v[lo:hi, g * hd:(g + 1) * hd]
            k_c = kc[:, g * hd:(g + 1) * hd]
            v_c = vc[:, g * hd:(g + 1) * hd]
            for r in range(2):
                h = 2 * g + r
                q_h = q[i * ABLOCK:(i + 1) * ABLOCK, h * hd:(h + 1) * hd]
                q_b = q_h.astype(bf16)
                s_loc = jnp.where(ok, _dot_nt(q_b, k_l.astype(bf16)) * scale, NEG_INF)
                s_ctx = _dot_nt(q_b, k_c.astype(bf16)) * scale
                outs.append(_sink_softmax_pv([s_loc, s_ctx], [v_l, v_c], sink_ref[0:1, h:h + 1]))
        y_ref[i * ABLOCK:(i + 1) * ABLOCK, :] = jnp.concatenate(outs, axis=1).astype(y_ref.dtype)


def latent_attention(u_att, t0, nseq, L, p, kc, vc, rope):
    b0 = t0 // L
    P = kc.shape[1]
    small = lambda s: pl.BlockSpec(s, lambda b: (0,) * len(s))
    return pl.pallas_call(
        functools.partial(_lat_att_body, L, P),
        grid=(nseq,),
        in_specs=[pl.BlockSpec((L, U_ATT), lambda b: (b0 + b, 0)),
                  pl.BlockSpec((1, P, 2 * HEAD_DIM), lambda b: (b, 0, 0)),
                  pl.BlockSpec((1, P, 2 * HEAD_DIM), lambda b: (b, 0, 0)),
                  small((1, MIX)), small((1, 2 * HEAD_DIM)), small((1, N_HEADS)),
                  small((L, MIX)), small((L, MIX)), small((L, 2 * HEAD_DIM)), small((L, 2 * HEAD_DIM))],
        out_specs=pl.BlockSpec((L, MIX), lambda b: (b, 0)),
        out_shape=jax.ShapeDtypeStruct((nseq * L, MIX), bf16),
        compiler_params=_params("arbitrary"),
        name="lat_att",
    )(u_att, kc, vc, p["q_norm"], p["k_norm"], p["sink"], *rope)


def rope_tables(L):
    rows = L // GRID_W
    row = jnp.repeat(jnp.arange(rows, dtype=f32), GRID_W)
    col = jnp.tile(jnp.arange(GRID_W, dtype=f32), rows)
    nf = HEAD_DIM // 4
    inv = ROPE_THETA ** (-jnp.arange(nf, dtype=f32) / nf)
    ang = jnp.concatenate([row[:, None] * inv, col[:, None] * inv], axis=-1)
    cos = jnp.concatenate([jnp.cos(ang), jnp.cos(ang)], axis=-1)
    sin = jnp.concatenate([-jnp.sin(ang), jnp.sin(ang)], axis=-1)
    return (jnp.tile(cos, (1, N_HEADS)), jnp.tile(sin, (1, N_HEADS)), jnp.tile(cos, (1, 2)), jnp.tile(sin, (1, 2)))


def _out_body(ya, yb, yc, yd, x_ref, w_ref, g1_ref, nw_ref, sc_ref, sh_ref, x1_ref, h2_ref, h2t_ref):
    y = _dot(ya[...], w_ref[0:MIX, :])
    y = y + _dot(yb[...], w_ref[MIX:2 * MIX, :])
    y = y + _dot(yc[...], w_ref[2 * MIX:3 * MIX, :])
    y = y + _dot(yd[...], w_ref[3 * MIX:4 * MIX, :])
    x1 = x_ref[...] + g1_ref[0] * y
    x1_ref[...] = x1
    h = x1 * lax.rsqrt(jnp.mean(x1 * x1, axis=-1, keepdims=True) + EPS) * nw_ref[...]
    h = h * (1.0 + sc_ref[0]) + sh_ref[0]
    h2_ref[...] = h.astype(bf16)
    h2t_ref[...] = h.T.astype(bf16)


def out_proj(ys, x, w_out, g1, nw, sc, sh, t_p, l_s):
    t = x.shape[0]
    tm = 512
    row = _mod_row_map(tm, t_p, l_s)
    tok = lambda w: pl.BlockSpec((tm, w), lambda i: (i, 0))
    return pl.pallas_call(
        _out_body,
        grid=(t // tm,),
        in_specs=[tok(MIX), tok(MIX), tok(MIX), tok(MIX), tok(D_MODEL),
                  pl.BlockSpec((D_MODEL, D_MODEL), lambda i: (0, 0)),
                  pl.BlockSpec((1, 1, D_MODEL), row),
                  pl.BlockSpec((1, D_MODEL), lambda i: (0, 0)),
                  pl.BlockSpec((1, 1, D_MODEL), row),
                  pl.BlockSpec((1, 1, D_MODEL), row)],
        out_specs=[tok(D_MODEL), tok(D_MODEL), pl.BlockSpec((D_MODEL, tm), lambda i: (0, i))],
        out_shape=[jax.ShapeDtypeStruct((t, D_MODEL), f32),
                   jax.ShapeDtypeStruct((t, D_MODEL), bf16),
                   jax.ShapeDtypeStruct((D_MODEL, t), bf16)],
        compiler_params=_params("arbitrary"),
        name="out_proj",
    )(*ys, x, w_out, g1, nw, sc, sh)


def _top16(s, n):
    io = lax.broadcasted_iota(jnp.int32, s.shape, 0).astype(f32)
    rank = jnp.full(s.shape, float(PEER_TOPK), f32)
    vals = []
    for r in range(PEER_TOPK):
        m = jnp.max(s, axis=0, keepdims=True)
        idx = jnp.min(jnp.where(s == m, io, float(n)), axis=0, keepdims=True)
        hit = io == idx
        rank = jnp.where(hit, float(r), rank)
        s = jnp.where(hit, -jnp.inf, s)
        vals.append(m)
    return jnp.concatenate(vals, axis=0), rank


def _peer_route_body(h2_ref, wq_ref, keys_ref, r2_ref, e2_ref, lim_ref, cw_ref):
    k = PEER_TOPK
    q = _dot(h2_ref[...], wq_ref[...])
    tn = q.shape[0]
    n_b = [k // (a + 1) for a in range(k)]
    n_cand = sum(n_b)
    n_rows = -(-n_cand // 8) * 8
    ra = lax.broadcasted_iota(jnp.int32, (k, n_rows), 0)
    rc = lax.broadcasted_iota(jnp.int32, (k, n_rows), 1)
    rep_a = jnp.zeros((k, n_rows), f32)
    start = 0
    for a in range(k):
        rep_a = jnp.where((ra == a) & (rc >= start) & (rc < start + n_b[a]), 1.0, rep_a)
        start += n_b[a]
    lw = 128
    pad = jnp.full((n_rows - n_cand, lw), -jnp.inf, f32)
    for h in range(PEER_HEADS):
        s1_all = _dot_nt(keys_ref[h, 0], q[:, h * 256:h * 256 + 128], HI)
        s2_all = _dot_nt(keys_ref[h, 1], q[:, h * 256 + 128:h * 256 + 256], HI)
        for j in range(tn // lw):
            cols = slice(j * lw, (j + 1) * lw)
            s1 = s1_all[:, cols]
            s2 = s2_all[:, cols]
            t1, rank1 = _top16(s1, PEER_NKEYS)
            t2, rank2 = _top16(s2, PEER_NKEYS)
            cand = jnp.concatenate([t1[a:a + 1, :] + t2[0:n_b[a], :] for a in range(k)] + [pad], axis=0)
            best, crank = _top16(cand, n_rows)
            sel = jnp.where(crank < k, 1.0, 0.0).astype(f32)
            count_a = _dot(rep_a, sel)
            zsum = jnp.sum(jnp.exp(best - best[0:1, :]), axis=0, keepdims=True)
            lim = jnp.zeros((PEER_NKEYS, lw), f32)
            for a in range(k):
                lim = jnp.where(rank1 == a, count_a[a:a + 1, :], lim)
            r2_ref[h, :, cols] = rank2.astype(bf16)
            e2_ref[h, :, cols] = jnp.exp(s2 - t2[0:1, :]).astype(bf16)
            lim_ref[h, :, cols] = lim
            cw_ref[h, :, cols] = jnp.exp(s1 - t1[0:1, :]) / zsum


def peer_route(h2, w_q, keys):
    t = h2.shape[0]
    tn = 256
    arr = jax.ShapeDtypeStruct((PEER_HEADS, PEER_NKEYS, t), f32)
    arr_h = jax.ShapeDtypeStruct((PEER_HEADS, PEER_NKEYS, t), bf16)
    spec = pl.BlockSpec((PEER_HEADS, PEER_NKEYS, tn), lambda i: (0, 0, i))
    return pl.pallas_call(
        _peer_route_body,
        grid=(t // tn,),
        in_specs=[pl.BlockSpec((tn, D_MODEL), lambda i: (i, 0)),
                  pl.BlockSpec((D_MODEL, 2 * D_MODEL), lambda i: (0, 0)),
                  pl.BlockSpec((PEER_HEADS, 2, PEER_NKEYS, PEER_NKEYS), lambda i: (0, 0, 0, 0))],
        out_specs=[spec, spec, spec, spec],
        out_shape=[arr_h, arr_h, arr, arr],
        compiler_params=_params("arbitrary"),
        name="peer_route",
    )(h2, w_q, keys)


def _peer_expert_body(eb_i, h2t_ref, u_ref, v_ref, r2_ref, e2_ref, lim_ref, cw_ref, x1_ref, g2_ref,
                      o_ref, wt, acc):
    e = pl.program_id(1)

    @pl.when(e == 0)
    def _():
        acc[...] = jnp.zeros_like(acc)

    sub_i = 4
    sub = sub_i * PEER_NKEYS
    n_sub = eb_i // sub_i

    def scores(c):
        return _dot(u_ref[pl.ds(c * sub, sub), :], h2t_ref[...])

    st_next = scores(0)
    for c in range(n_sub):
        rows = pl.ds(c * sub, sub)
        st = st_next
        if c + 1 < n_sub:
            st_next = scores(c + 1)
        for ii in range(sub_i):
            i = e * eb_i + c * sub_i + ii
            g = None
            for h in range(PEER_HEADS):
                lim = lim_ref[h, pl.ds(i, 1), :].astype(bf16)
                cw = cw_ref[h, pl.ds(i, 1), :].astype(bf16)
                term = jnp.where(r2_ref[h] < lim, e2_ref[h] * cw, jnp.zeros((), bf16))
                g = term if g is None else g + term
            s = st[ii * PEER_NKEYS:(ii + 1) * PEER_NKEYS, :]
            act = 0.5 * s * (1.0 + lax.erf(s * (2.0 ** -0.5)))
            wt[pl.ds(c * sub + ii * PEER_NKEYS, PEER_NKEYS), :] = g * act.astype(bf16)
        acc[...] += _dot(v_ref[:, c * sub:(c + 1) * sub], wt[rows, :])

    @pl.when(e == pl.num_programs(1) - 1)
    def _():
        o_ref[...] = x1_ref[...] + g2_ref[0] * acc[...].T


def peer_experts(h2t, u_tab, v_tab, route, x1, g2, t_p, l_s):
    t = x1.shape[0]
    tn = 512
    eb_i = 16
    eb = eb_i * PEER_NKEYS
    row = _mod_row_map(tn, t_p, l_s)
    rspec = pl.BlockSpec((PEER_HEADS, PEER_NKEYS, tn), lambda j, e: (0, 0, j))
    return pl.pallas_call(
        functools.partial(_peer_expert_body, eb_i),
        grid=(t // tn, PEER_EXPERTS // eb),
        in_specs=[pl.BlockSpec((D_MODEL, tn), lambda j, e: (0, j)),
                  pl.BlockSpec((eb, D_MODEL), lambda j, e: (e, 0)),
                  pl.BlockSpec((D_MODEL, eb), lambda j, e: (0, e)),
                  rspec, rspec, rspec, rspec,
                  pl.BlockSpec((tn, D_MODEL), lambda j, e: (j, 0)),
                  pl.BlockSpec((1, 1, D_MODEL), lambda j, e: row(j))],
        out_specs=pl.BlockSpec((tn, D_MODEL), lambda j, e: (j, 0)),
        out_shape=jax.ShapeDtypeStruct((t, D_MODEL), f32),
        scratch_shapes=[pltpu.VMEM((eb, tn), bf16), pltpu.VMEM((D_MODEL, tn), f32)],
        compiler_params=_params("arbitrary", "arbitrary"),
        name="peer_experts",
    )(h2t, u_tab, v_tab, *route, x1, g2)


def _regroup_w_in(w):
    o_gdn = 776
    o_fn = o_gdn + 1040
    o_att = o_fn + 256
    main = jnp.concatenate([w[:, 0:768], w[:, o_gdn:o_gdn + 1024], w[:, o_fn:o_fn + 256], w[:, o_att:o_att + 512]], axis=1)
    small = jnp.concatenate([w[:, 768:776], w[:, o_gdn + 1024:o_gdn + 1040],
                             jnp.zeros((w.shape[0], U_SM - 24), w.dtype)], axis=1)
    return main.astype(bf16), small, small.T


def _layer_params(i, ssd_conv_w, ssd_conv_b, ssd_A_log, ssd_dt_bias, ssd_D, ssd_norm_w,
                  gdn_conv_w, gdn_conv_b, gdn_A_log, gdn_dt_bias, gdn_norm_w, q_norm_w, k_norm_w, att_sink):
    ssd = dict(conv_w=ssd_conv_w[i], conv_b=ssd_conv_b[i][None, :],
               al_row=ssd_A_log[i].reshape(1, 8), al_col=ssd_A_log[i].reshape(8, 1),
               db_row=ssd_dt_bias[i].reshape(1, 8), db_col=ssd_dt_bias[i].reshape(8, 1),
               d_skip=jnp.repeat(ssd_D[i], HEAD_DIM)[None, :], norm_w=ssd_norm_w[i][None, :])
    gdn = dict(conv_w=gdn_conv_w[i], conv_b=gdn_conv_b[i][None, :],
               al_row=gdn_A_log[i].reshape(1, 8), al_col=gdn_A_log[i].reshape(8, 1),
               db_row=gdn_dt_bias[i].reshape(1, 8), db_col=gdn_dt_bias[i].reshape(8, 1),
               norm_w=jnp.tile(gdn_norm_w[i], N_HEADS)[None, :])
    att = dict(q_norm=jnp.tile(q_norm_w[i], N_HEADS)[None, :], k_norm=jnp.tile(k_norm_w[i], 2)[None, :],
               sink=att_sink[i][None, :])
    return ssd, gdn, att


def kernel(x_prompt, x_sample, cache_k, cache_v, state_ssd, state_gdn, c, c_ctx, w_mod, b_mod, norm1_w, norm2_w, w_in, w_out, ssd_conv_w, ssd_conv_b, ssd_A_log, ssd_dt_bias, ssd_D, ssd_norm_w, gdn_conv_w, gdn_conv_b, gdn_A_log, gdn_dt_bias, gdn_norm_w, q_norm_w, k_norm_w, att_sink, peer_w_q, peer_keys, peer_u, peer_v):
    nb_p, l_p, d = x_prompt.shape
    nb_s, l_s, _ = x_sample.shape
    past = cache_k.shape[2]
    t_p = nb_p * l_p
    x = jnp.concatenate([x_prompt.reshape(t_p, d), x_sample.reshape(nb_s * l_s, d)], axis=0)

    cvecs = jnp.concatenate([c_ctx[None, :], c, jnp.zeros((8 - 1 - nb_s, d), f32)], axis=0)
    mods = adaln_all(cvecs, w_mod, b_mod)
    dft_p, dft_s = dft_tables(l_p), dft_tables(l_s)
    rope = rope_tables(l_s)
    zeros_state = jnp.zeros((nb_p, 2, N_HEADS, HEAD_DIM, HEAD_DIM), f32)

    ks, vs, ssd_states, gdn_states = [], [], [], []
    for i in range(DEPTH):
        sh1, sc1, g1, sh2, sc2, g2 = [mods[i, :, j * d:(j + 1) * d].reshape(8, 1, d) for j in range(6)]
        ssd_p, gdn_p, att_p = _layer_params(i, ssd_conv_w, ssd_conv_b, ssd_A_log, ssd_dt_bias, ssd_D, ssd_norm_w,
                                            gdn_conv_w, gdn_conv_b, gdn_A_log, gdn_dt_bias, gdn_norm_w,
                                            q_norm_w, k_norm_w, att_sink)
        w_main, w_sm, w_smt = _regroup_w_in(w_in[i])
        u_ssd, u_gdn, u_fn, u_att, u_sm, u_smt = in_proj(x, norm1_w[i][None, :], sc1, sh1, w_main, w_sm, w_smt, t_p, l_s)

        y_ssd_p, st_ssd = ssd_mixer(u_ssd, u_sm, u_smt, 0, nb_p, l_p, ssd_p, zeros_state)
        y_ssd_s, _ = ssd_mixer(u_ssd, u_sm, u_smt, t_p, nb_s, l_s, ssd_p, state_ssd[:, i])
        y_gdn_p, st_gdn = gdn_mixer(u_gdn, u_sm, u_smt, 0, nb_p, l_p, gdn_p, zeros_state)
        y_gdn_s, _ = gdn_mixer(u_gdn, u_sm, u_smt, t_p, nb_s, l_s, gdn_p, state_gdn[:, i])
        y_fn_p = fnet_mixer(u_fn, 0, nb_p, l_p, dft_p)
        y_fn_s = fnet_mixer(u_fn, t_p, nb_s, l_s, dft_s)
        y_att_p, k_new, v_new = ctx_attention(u_att, 0, nb_p, l_p, att_p)
        y_att_s = latent_attention(u_att, t_p, nb_s, l_s, att_p,
                                   cache_k[:, i].reshape(nb_s, past, 2 * HEAD_DIM),
                                   cache_v[:, i].reshape(nb_s, past, 2 * HEAD_DIM), rope)
        ys = [jnp.concatenate([a, b], axis=0) for a, b in
              ((y_ssd_p, y_ssd_s), (y_gdn_p, y_gdn_s), (y_fn_p, y_fn_s), (y_att_p, y_att_s))]

        x1, h2, h2t = out_proj(ys, x, w_out[i].astype(bf16), g1, norm2_w[i][None, :], sc2, sh2, t_p, l_s)
        route = peer_route(h2, peer_w_q[i].astype(bf16), peer_keys[i])
        x = peer_experts(h2t, peer_u[i].astype(bf16), peer_v[i].astype(bf16).T, route, x1, g2, t_p, l_s)

        ks.append(k_new.reshape(nb_p, l_p, 2, HEAD_DIM))
        vs.append(v_new.reshape(nb_p, l_p, 2, HEAD_DIM))
        ssd_states.append(st_ssd)
        gdn_states.append(st_gdn)

    return (x[:t_p].reshape(nb_p, l_p, d), x[t_p:].reshape(nb_s, l_s, d),
            jnp.stack(ks, axis=1), jnp.stack(vs, axis=1),
            jnp.stack(ssd_states, axis=1), jnp.stack(gdn_states, axis=1))
```
